```python
import math
import jax, jax.numpy as jnp
from jax import lax
import numpy as np

D_MODEL = 4096
BATCH = 1
SEQ = 8192
DEPTH = 4

MIX_W = D_MODEL
W_A = MIX_W // 4
W_B = MIX_W // 4
W_C = MIX_W // 4
W_D = MIX_W - W_A - W_B - W_C
RW_HEAD = 64
RW_H = W_A // RW_HEAD
RW_LORA = 64
RW_DECAY_SCALE = 0.606531
RW_GN_EPS = 64e-5
ML_H = 4
ML_HD = W_B // ML_H
HG_HD = 128
HG_H = W_C // HG_HD
HG_VD = W_C // HG_H
GD_HD = 128
GD_H = W_D // GD_HD
CONV_K = 4
CHUNK = 64
NORM_EPS = 1e-6
L2_EPS = 1e-6

A_SIZES = (W_A, W_A, W_A, RW_LORA, RW_LORA, W_A)
A_SHIFT = 3 * W_A + 2 * RW_LORA
B_SIZES = (W_B, W_B, W_B, W_B, ML_H, ML_H, W_B)
C_SIZES = (W_C, W_C, W_C, W_C)
D_SIZES = (W_D, W_D, W_D, GD_H, GD_H, W_D)
GROUP_SIZES = (sum(A_SIZES), sum(B_SIZES), sum(C_SIZES), sum(D_SIZES))
P_IN = sum(GROUP_SIZES)

kernel_name = 'hybrid_rwkv7_mlstm_hgrn2_gdn_trunk'


def split_cols(y, sizes):
    idx = [int(i) for i in np.cumsum(sizes)[:-1]]
    return jnp.split(y, idx, axis=-1)


def rms_norm(x, g):
    xf = x.astype(jnp.float32)
    y = xf * lax.rsqrt(jnp.mean(xf * xf, -1, keepdims=True) + NORM_EPS)
    return (y * g.astype(jnp.float32)).astype(x.dtype)


def head_rms_norm(o, g):
    o = o * lax.rsqrt(jnp.mean(o * o, -1, keepdims=True) + NORM_EPS)
    return o.reshape(o.shape[0], o.shape[1], -1) * g


def group_norm(o, g, b):
    mean = jnp.mean(o, -1, keepdims=True)
    var = jnp.mean(jnp.square(o - mean), -1, keepdims=True)
    o = (o - mean) * lax.rsqrt(var + RW_GN_EPS)
    return o.reshape(o.shape[0], o.shape[1], -1) * g + b


def l2norm(t):
    return t * lax.rsqrt(jnp.sum(t * t, -1, keepdims=True) + L2_EPS)


def to_heads(t, n_heads):
    return t.reshape(t.shape[0], t.shape[1], n_heads, -1)


def to_chunks(t):
    b, s = t.shape[:2]
    t = t.reshape(b, s // CHUNK, CHUNK, *t.shape[2:])
    return jnp.swapaxes(jnp.moveaxis(t, 1, 0), 2, 3)


def from_chunks(t):
    t = jnp.moveaxis(jnp.swapaxes(t, 2, 3), 0, 1)
    return t.reshape(t.shape[0], -1, *t.shape[3:])


def causal_conv(x, w):
    return lax.conv_general_dilated(
        x, w[:, None, :].astype(x.dtype), window_strides=(1,), padding=[(CONV_K - 1, 0)],
        dimension_numbers=('NWC', 'WIO', 'NWC'), feature_group_count=x.shape[-1])


def token_shift(y, mu):
    prev = jnp.pad(y, ((0, 0), (1, 0), (0, 0)))[:, :-1]
    return y + (prev - y) * mu


def rwkv7_branch(y, mu, w0, w_up, a0, a_up, k_k, k_a, r_k, ln_g, ln_b):
    bsz, seq = y.shape[:2]
    r, k, v, w_lo, a_lo = split_cols(token_shift(y[..., :A_SHIFT], mu), A_SIZES[:5])
    z = y[..., A_SHIFT:]
    log_w = -RW_DECAY_SCALE * jax.nn.sigmoid(w0 + jnp.tanh(w_lo) @ w_up)
    a = jax.nn.sigmoid(a0 + a_lo @ a_up)
    kk = l2norm(to_heads(k * k_k, RW_H))
    k = k * (1.0 + (a - 1.0) * k_a)
    r, k, v, a, decay = (to_heads(t, RW_H) for t in (r, k, v, a, jnp.exp(log_w)))

    def step(S, inp):
        r_t, w_t, k_t, v_t, kk_t, b_t = inp
        S = (S * w_t[:, :, None, :]
             - jnp.einsum('bhvk,bhk->bhv', S, kk_t)[..., None] * b_t[:, :, None, :]
             + v_t[..., None] * k_t[:, :, None, :])
        return S, jnp.einsum('bhvk,bhk->bhv', S, r_t)

    S0 = jnp.zeros((bsz, RW_H, RW_HEAD, RW_HEAD), jnp.float32)
    _, o = lax.scan(step, S0, tuple(jnp.moveaxis(t, 1, 0) for t in (r, decay, k, v, kk, kk * a)))
    o = group_norm(jnp.moveaxis(o, 0, 1), ln_g, ln_b)
    o = o + (jnp.sum(r * k * r_k, -1, keepdims=True) * v).reshape(bsz, seq, W_A)
    return o * jax.nn.silu(z)


def mlstm_branch(y, conv_w, i_bias, f_bias, norm_g):
    bsz = y.shape[0]
    q, k, v, og, ip, fp, z = split_cols(y, B_SIZES)
    q, k = jnp.split(jax.nn.silu(causal_conv(jnp.concatenate([q, k], -1), conv_w)), 2, axis=-1)
    q = to_heads(q, ML_H)
    k = to_heads(k, ML_H) * ML_HD ** -0.5
    v = to_heads(v, ML_H)
    log_i = ip + i_bias
    log_f = jax.nn.log_sigmoid(fp + f_bias)
    causal = jnp.tril(jnp.ones((CHUNK, CHUNK), bool))

    def step(carry, inp):
        C, n, m = carry
        qc, kc, vc, ic, fc = inp
        g = jnp.cumsum(fc, axis=-1)
        D = jnp.where(causal, g[..., :, None] - g[..., None, :] + ic[..., None, :], -jnp.inf)
        inter = g + m[..., None]
        m_row = jnp.maximum(jnp.max(D, -1), inter)
        s = jnp.einsum('bhld,bhsd->bhls', qc, kc) * jnp.exp(D - m_row[..., None])
        w_inter = jnp.exp(inter - m_row)
        num = jnp.einsum('bhls,bhse->bhle', s, vc) + w_inter[..., None] * jnp.einsum('bhld,bhde->bhle', qc, C)
        den = jnp.sum(s, -1) + w_inter * jnp.einsum('bhld,bhd->bhl', qc, n)
        h = num / jnp.maximum(jnp.abs(den), jnp.exp(-m_row))[..., None]
        g_end = g[..., -1]
        log_w = g_end[..., None] - g + ic
        m_new = jnp.maximum(g_end + m, jnp.max(log_w, -1))
        carry_scale = jnp.exp(g_end + m - m_new)
        wk = jnp.exp(log_w - m_new[..., None])[..., None] * kc
        C = carry_scale[..., None, None] * C + jnp.einsum('bhsd,bhse->bhde', wk, vc)
        n = carry_scale[..., None] * n + jnp.sum(wk, axis=2)
        return (C, n, m_new), h

    init = (jnp.zeros((bsz, ML_H, ML_HD, ML_HD), jnp.float32),
            jnp.zeros((bsz, ML_H, ML_HD), jnp.float32),
            jnp.zeros((bsz, ML_H), jnp.float32))
    _, h = lax.scan(step, init, tuple(to_chunks(t) for t in (q, k, v, log_i, log_f)))
    h = head_rms_norm(from_chunks(h), norm_g) * jax.nn.sigmoid(og)
    return h * jax.nn.silu(z)


def hgrn2_branch(y, lower_bound, norm_g):
    bsz = y.shape[0]
    q, fp, i, z = split_cols(y, C_SIZES)
    f = lower_bound + (1.0 - lower_bound) * jax.nn.sigmoid(fp)
    q = to_heads(jax.nn.silu(q), HG_H)
    k = to_heads(1.0 - f, HG_H)
    log_f = to_heads(jnp.log(f), HG_H)
    v = to_heads(i, HG_H)
    causal = jnp.tril(jnp.ones((CHUNK, CHUNK), bool))[:, :, None]

    def step(S, inp):
        qc, kc, vc, fc = inp
        b = jnp.cumsum(fc, axis=2)
        decay = jnp.exp(jnp.where(causal, b[:, :, :, None, :] - b[:, :, None, :, :], -jnp.inf))
        A = jnp.einsum('bhjc,bhlc,bhjlc->bhjl', qc, kc, decay)
        o = jnp.einsum('bhjl,bhlv->bhjv', A, vc) + jnp.einsum('bhjc,bhcv->bhjv', qc * jnp.exp(b), S)
        b_end = b[:, :, -1]
        S = jnp.exp(b_end)[..., None] * S + jnp.einsum('bhlc,bhlv->bhcv', kc * jnp.exp(b_end[:, :, None] - b), vc)
        return S, o

    S0 = jnp.zeros((bsz, HG_H, HG_HD, HG_VD), jnp.float32)
    _, o = lax.scan(step, S0, tuple(to_chunks(t) for t in (q, k, v, log_f)))
    return head_rms_norm(from_chunks(o), norm_g) * jax.nn.silu(z)


def gated_deltanet_branch(y, conv_w, A_log, dt_bias, norm_g):
    bsz = y.shape[0]
    q, k, v, bp, ap, z = split_cols(y, D_SIZES)
    q, k, v = jnp.split(jax.nn.silu(causal_conv(jnp.concatenate([q, k, v], -1), conv_w)), 3, axis=-1)
    q = l2norm(to_heads(q, GD_H)) * GD_HD ** -0.5
    k = l2norm(to_heads(k, GD_H))
    v = to_heads(v, GD_H)
    beta = jax.nn.sigmoid(bp)
    log_a = -jnp.exp(A_log) * jax.nn.softplus(ap + dt_bias)
    incl = jnp.tril(jnp.ones((CHUNK, CHUNK), bool))
    strict = jnp.tril(jnp.ones((CHUNK, CHUNK), bool), k=-1)
    eye = jnp.eye(CHUNK, dtype=jnp.float32)

    def step(S, inp):
        qc, kc, vc, bc, gc = inp
        g = jnp.cumsum(gc, -1)
        decay = jnp.exp(jnp.where(incl, g[..., :, None] - g[..., None, :], -jnp.inf))
        kb = kc * bc[..., None]
        M = jnp.where(strict, jnp.einsum('bhid,bhjd->bhij', kb, kc) * decay, 0.0)
        rhs = jnp.concatenate([vc * bc[..., None], kb * jnp.exp(g)[..., None]], -1)
        sol = lax.linalg.triangular_solve(eye + M, rhs, left_side=True, lower=True)
        u, w = sol[..., :GD_HD], sol[..., GD_HD:]
        v_new = u - jnp.einsum('bhid,bhde->bhie', w, S)
        attn = jnp.einsum('bhid,bhjd->bhij', qc, kc) * decay
        o = jnp.einsum('bhid,bhde->bhie', qc * jnp.exp(g)[..., None], S) + jnp.einsum('bhij,bhje->bhie', attn, v_new)
        g_end = g[..., -1]
        S = jnp.exp(g_end)[..., None, None] * S + jnp.einsum('bhjd,bhje->bhde', kc * jnp.exp(g_end[..., None] - g)[..., None], v_new)
        return S, o

    S0 = jnp.zeros((bsz, GD_H, GD_HD, GD_HD), jnp.float32)
    _, o = lax.scan(step, S0, tuple(to_chunks(t) for t in (q, k, v, beta, log_a)))
    return head_rms_norm(from_chunks(o), norm_g) * jax.nn.silu(z)


def setup_inputs(seed: int = 0) -> dict:
    key = jax.random.key(seed)
    k = jax.random.split(key, 28)
    f32 = jnp.float32

    def nrm(i, shape, scale):
        return jax.random.normal(k[i], shape, f32) * scale

    def unif(i, shape, lo, hi):
        return jax.random.uniform(k[i], shape, f32, lo, hi)

    dt = jnp.exp(unif(25, (DEPTH, GD_H), math.log(1e-3), math.log(1e-1)))
    return {
        'x': nrm(0, (BATCH, SEQ, D_MODEL), 1.0),
        'c': nrm(1, (BATCH, D_MODEL), 1.0),
        'norm_g': 1.0 + nrm(2, (DEPTH, D_MODEL), 0.02),
        'w_ada': nrm(3, (DEPTH, D_MODEL, 3 * D_MODEL), D_MODEL ** -0.5),
        'b_ada': nrm(4, (DEPTH, 3 * D_MODEL), 0.02),
        'w_in': nrm(5, (DEPTH, D_MODEL, P_IN), D_MODEL ** -0.5),
        'w_out': nrm(6, (DEPTH, MIX_W, D_MODEL), MIX_W ** -0.5),
        'rw_mu': unif(7, (DEPTH, A_SHIFT), 0.0, 1.0),
        'rw_w0': nrm(8, (DEPTH, W_A), 0.5),
        'rw_w_up': nrm(9, (DEPTH, RW_LORA, W_A), RW_LORA ** -0.5),
        'rw_a0': nrm(10, (DEPTH, W_A), 0.1),
        'rw_a_up': nrm(11, (DEPTH, RW_LORA, W_A), RW_LORA ** -0.5),
        'rw_k_k': 0.85 + nrm(12, (DEPTH, W_A), 0.02),
        'rw_k_a': 1.0 + nrm(13, (DEPTH, W_A), 0.02),
        'rw_r_k': nrm(14, (DEPTH, RW_H, RW_HEAD), 0.1),
        'rw_ln_g': 1.0 + nrm(15, (DEPTH, W_A), 0.02),
        'rw_ln_b': nrm(16, (DEPTH, W_A), 0.02),
        'ml_conv': nrm(17, (DEPTH, CONV_K, 2 * W_B), CONV_K ** -0.5),
        'ml_i_bias': nrm(18, (DEPTH, ML_H), 0.1),
        'ml_f_bias': jnp.linspace(3.0, 6.0, ML_H, dtype=f32)[None, :] + nrm(19, (DEPTH, ML_H), 0.1),
        'ml_norm_g': 1.0 + nrm(20, (DEPTH, W_B), 0.02),
        'hg_lower': nrm(21, (DEPTH, W_C), 0.1),
        'hg_norm_g': 1.0 + nrm(22, (DEPTH, W_C), 0.02),
        'gd_conv': nrm(23, (DEPTH, CONV_K, 3 * W_D), CONV_K ** -0.5),
        'gd_A_log': jnp.log(unif(24, (DEPTH, GD_H), 1.0, 16.0)),
        'gd_dt_bias': dt + jnp.log(-jnp.expm1(-dt)),
        'gd_norm_g': 1.0 + nrm(26, (DEPTH, W_D), 0.02),
        'final_g': 1.0 + nrm(27, (D_MODEL,), 0.02),
    }


def reference(x, c, norm_g, w_ada, b_ada, w_in, w_out, rw_mu, rw_w0, rw_w_up, rw_a0, rw_a_up, rw_k_k,
              rw_k_a, rw_r_k, rw_ln_g, rw_ln_b, ml_conv, ml_i_bias, ml_f_bias, ml_norm_g, hg_lower,
              hg_norm_g, gd_conv, gd_A_log, gd_dt_bias, gd_norm_g, final_g):
    cum = jnp.cumsum(jax.nn.softmax(hg_lower.astype(jnp.float32), axis=0), axis=0)
    lower_bounds = cum - cum[0]
    c_act = jax.nn.silu(c)
    for l in range(DEPTH):
        shift, scale, gate = jnp.split(c_act @ w_ada[l] + b_ada[l], 3, axis=-1)
        h = rms_norm(x, norm_g[l]) * (1.0 + scale[:, None, :]) + shift[:, None, :]
        y = (h @ w_in[l]).astype(jnp.float32)
        y_a, y_b, y_c, y_d = split_cols(y, GROUP_SIZES)
        o = jnp.concatenate([
            rwkv7_branch(y_a, rw_mu[l], rw_w0[l], rw_w_up[l], rw_a0[l], rw_a_up[l], rw_k_k[l], rw_k_a[l],
                         rw_r_k[l], rw_ln_g[l], rw_ln_b[l]),
            mlstm_branch(y_b, ml_conv[l], ml_i_bias[l], ml_f_bias[l], ml_norm_g[l]),
            hgrn2_branch(y_c, lower_bounds[l], hg_norm_g[l]),
            gated_deltanet_branch(y_d, gd_conv[l], gd_A_log[l], gd_dt_bias[l], gd_norm_g[l]),
        ], axis=-1).astype(x.dtype)
        x = x + gate[:, None, :] * (o @ w_out[l])
    return rms_norm(x, final_g)
```

```python
import functools

import jax
import jax.numpy as jnp
from jax import lax
from jax.experimental import pallas as pl
from jax.experimental.pallas import tpu as pltpu

F32 = jnp.float32
BF16 = jnp.bfloat16
HIGHEST = lax.Precision.HIGHEST

D_MODEL = 4096
DEPTH = 4
W_MIX = 1024
RW_HEAD = 64
RW_LORA = 64
RW_DECAY_SCALE = 0.606531
RW_GN_EPS = 64e-5
ML_H = 4
ML_HD = 256
HG_H = 8
HG_HD = 128
GD_H = 8
GD_HD = 128
CONV_K = 4
CHUNK = 64
_LOG2_CHUNK = 6
_LOG2_RW_HEAD = 6
assert 1 << _LOG2_CHUNK == CHUNK and 1 << _LOG2_RW_HEAD == RW_HEAD
NORM_EPS = 1e-6
L2_EPS = 1e-6

LANES = 128
SUBLANES = 8
VMEM_LIMIT = 48 * 1024 * 1024

_COL = dict(
    B_q=0, B_k=1024, B_v=2048, B_o=3072, B_z=4096,
    C_q=5120, C_f=6144, C_i=7168, C_z=8192,
    A_r=9216, A_k=10240, A_v=11264, A_z=12288,
    D_q=13312, D_k=14336, D_v=15360, D_z=16384,
    A_lo=17408, B_g=17536, D_g=17664,
)
P_PAD = 17920
_A0, _B0, _C0, _D0 = 0, 4224, 9352, 13448
_SRC = (
    ("B_q", _B0, 1024), ("B_k", _B0 + 1024, 1024), ("B_v", _B0 + 2048, 1024),
    ("B_o", _B0 + 3072, 1024), ("B_z", _B0 + 4104, 1024),
    ("C_q", _C0, 1024), ("C_f", _C0 + 1024, 1024), ("C_i", _C0 + 2048, 1024),
    ("C_z", _C0 + 3072, 1024),
    ("A_r", _A0, 1024), ("A_k", _A0 + 1024, 1024), ("A_v", _A0 + 2048, 1024),
    ("A_z", _A0 + 3200, 1024),
    ("D_q", _D0, 1024), ("D_k", _D0 + 1024, 1024), ("D_v", _D0 + 2048, 1024),
    ("D_z", _D0 + 3088, 1024),
    ("A_lo", _A0 + 3072, 128), ("B_g", _B0 + 4096, 8), ("D_g", _D0 + 3072, 16),
)


def relayout_columns(w):
  parts, pos = [], 0
  for name, src, width in _SRC:
    assert _COL[name] == pos, (name, pos)
    parts.append(w[..., src:src + width])
    pad = (-width) % LANES
    if pad:
      parts.append(jnp.zeros(w.shape[:-1] + (pad,), w.dtype))
    pos += width + pad
  parts.append(jnp.zeros(w.shape[:-1] + (P_PAD - pos,), w.dtype))
  return jnp.concatenate(parts, axis=-1)


def _iota(shape, dim):
  return lax.broadcasted_iota(jnp.int32, shape, dim)


def _dot(a, b, precision=HIGHEST):
  return jnp.dot(a, b, precision=precision, preferred_element_type=F32)


def _dot_nt(a, b, precision=HIGHEST):
  return lax.dot_general(a, b, (((1,), (1,)), ((), ())), precision=precision,
                         preferred_element_type=F32)


def _dot_tn(a, b, precision=HIGHEST):
  return lax.dot_general(a, b, (((0,), (0,)), ((), ())), precision=precision,
                         preferred_element_type=F32)


def _sigmoid(x):
  return 1.0 / (1.0 + jnp.exp(-x))


def _silu(x):
  return x * _sigmoid(x)


def _softplus(x):
  return jnp.maximum(x, 0.0) + jnp.log1p(jnp.exp(-jnp.abs(x)))


def _shift_rows(x, prev8, j):
  rolled = pltpu.roll(x, j, 0)
  top = jnp.where(_iota((SUBLANES, x.shape[1]), 0) < j, pltpu.roll(prev8, j, 0),
                  rolled[:SUBLANES])
  return jnp.concatenate([top, rolled[SUBLANES:]], axis=0)


def _causal_conv_silu(x_ref, w_ref, prev_ref, slot):
  x = x_ref[...]
  prev8 = prev_ref[slot]
  acc = x * w_ref[CONV_K - 1:CONV_K, :]
  for j in range(1, CONV_K):
    acc = acc + _shift_rows(x, prev8, j) * w_ref[CONV_K - 1 - j:CONV_K - j, :]
  prev_ref[slot] = x[x.shape[0] - SUBLANES:]
  return _silu(acc)


def _tri(n):
  r, c = _iota((n, n), 0), _iota((n, n), 1)
  return r, c


def _inv_unit_lower(a, n_factors):
  n = a.shape[0]
  r, c = _tri(n)
  x = jnp.where(r == c, 1.0, 0.0) - a
  p = _dot(a, a)
  for i in range(n_factors):
    x = x + _dot(x, p)
    if i + 1 < n_factors:
      p = _dot(p, p)
  return x


def _select_lane(x, lane_idx):
  n = x.shape[1]
  sel = jnp.where(_iota((n, LANES), 0) == lane_idx, 1.0, 0.0)
  return _dot(x, sel)


def _head_rms(o, g):
  return o * lax.rsqrt(jnp.mean(o * o, axis=-1, keepdims=True) + NORM_EPS) * g


def _rwkv_kernel(r_ref, k_ref, v_ref, lo_ref, z_ref, mur_ref, muk_ref, muv_ref, mulo_ref,
                 w0_ref, a0_ref, kkw_ref, kaw_ref, rkw_ref, lng_ref, lnb_ref, wup_ref, aup_ref,
                 o_ref, s_ref, prev_ref):
  bt = r_ref.shape[0]
  n = CHUNK
  n2 = 2 * n

  @pl.when(pl.program_id(1) == 0)
  def _():
    s_ref[...] = jnp.zeros_like(s_ref)
    prev_ref[...] = jnp.zeros_like(prev_ref)

  def token_shift(x_ref, mu_ref, slot):
    x = x_ref[...]
    prev = _shift_rows(x, prev_ref[slot], 1)
    prev_ref[slot] = x[bt - SUBLANES:]
    return x + (prev - x) * mu_ref[...]

  r = token_shift(r_ref, mur_ref, 0)
  k = token_shift(k_ref, muk_ref, 1)
  v = token_shift(v_ref, muv_ref, 2)
  lo = token_shift(lo_ref, mulo_ref, 3)

  lane = _iota((bt, LANES), 1)
  lo_act = jnp.where(lane < RW_LORA, jnp.tanh(lo), lo)
  log_w = -RW_DECAY_SCALE * _sigmoid(w0_ref[...] + _dot(lo_act, wup_ref[...]))
  a = _sigmoid(a0_ref[...] + _dot(lo_act, aup_ref[...]))

  hr, hc = _tri(LANES)
  head_ones = jnp.where(hr >> _LOG2_RW_HEAD == hc >> _LOG2_RW_HEAD, 1.0, 0.0)
  kkp = k * kkw_ref[...]
  kk = kkp * lax.rsqrt(_dot(kkp * kkp, head_ones) + L2_EPS)
  k2 = k * (1.0 + (a - 1.0) * kaw_ref[...])
  b = kk * a
  bonus = _dot(r * k2 * rkw_ref[...], head_ones) * v

  m0 = jnp.where(_iota((n, LANES), 1) < RW_HEAD, 1.0, 0.0)
  m1 = 1.0 - m0

  def stack(x):
    return jnp.concatenate([x * m0, x * m1], axis=0)

  tr, tc = _tri(n)
  lower_incl = jnp.where(tc <= tr, 1.0, 0.0)
  sr, sc = _tri(n2)
  same = (sr >> _LOG2_CHUNK) == (sc >> _LOG2_CHUNK)
  strict = same & (sc < sr)
  incl = same & (sc <= sr)

  s = s_ref[...]
  outs = []
  for ci in range(bt // n):
    sl = slice(ci * n, (ci + 1) * n)
    lw_c = log_w[sl]
    lc = _dot(lower_incl, lw_c)
    g_in = jnp.exp(lc)
    g_last = g_in[n - 1:n]
    kap = stack(kk[sl] * jnp.exp(lc - lw_c))
    rt = stack(r[sl] * g_in)
    g_inv = jnp.exp(-lc)
    kh = k2[sl] * g_inv
    bh = b[sl] * g_inv
    khs, bhs, vs = stack(kh), stack(bh), stack(v[sl])
    a_bk = jnp.where(strict, _dot_nt(kap, bhs), 0.0)
    a_kk = jnp.where(strict, _dot_nt(kap, khs), 0.0)
    a_kr = jnp.where(incl, _dot_nt(rt, khs), 0.0)
    a_br = jnp.where(incl, _dot_nt(rt, bhs), 0.0)
    t_inv = _inv_unit_lower(a_bk, 5)
    u = _dot(t_inv, _dot_nt(kap, s) + _dot(a_kk, vs))
    o = _dot_nt(rt, s) + _dot(a_kr, vs) - _dot(a_br, u)
    s = s * g_last + _dot_tn(vs, stack(kh * g_last)) - _dot_tn(u, stack(bh * g_last))
    outs.append(o[:n] + o[n:])
  s_ref[...] = s
  o = jnp.concatenate(outs, axis=0)

  inv_n = 1.0 / RW_HEAD
  mean = _dot(o, head_ones) * inv_n
  d = o - mean
  var = _dot(d * d, head_ones) * inv_n
  o = d * lax.rsqrt(var + RW_GN_EPS) * lng_ref[...] + lnb_ref[...] + bonus
  o_ref[...] = (o * _silu(z_ref[...])).astype(o_ref.dtype)


def rwkv7_mixer(y, mu, w0, w_up, a0, a_up, k_k, k_a, r_k, ln_g, ln_b, *, bt):
  t = y.shape[0]
  pairs = W_MIX // LANES
  cb = lambda name: _COL[name] // LANES
  col = lambda name: pl.BlockSpec((bt, LANES), lambda h, i, c=cb(name): (i, c + h))
  vec = lambda: pl.BlockSpec((1, LANES), lambda h, i: (0, h))
  fix = lambda shape: pl.BlockSpec(shape, lambda h, i: (0, 0))
  zeros = jnp.zeros((RW_LORA, W_MIX), F32)
  wup_p = jnp.concatenate([w_up, zeros], axis=0)
  aup_p = jnp.concatenate([zeros, a_up], axis=0)
  row = lambda p: p.reshape(1, -1)
  return pl.pallas_call(
      _rwkv_kernel,
      grid=(pairs, t // bt),
      in_specs=[
          col("A_r"), col("A_k"), col("A_v"),
          pl.BlockSpec((bt, LANES), lambda h, i, c=cb("A_lo"): (i, c)),
          col("A_z"),
          vec(), pl.BlockSpec((1, LANES), lambda h, i: (0, pairs + h)),
          pl.BlockSpec((1, LANES), lambda h, i: (0, 2 * pairs + h)),
          pl.BlockSpec((1, LANES), lambda h, i: (0, 3 * pairs)),
          vec(), vec(), vec(), vec(), vec(), vec(), vec(),
          pl.BlockSpec((LANES, LANES), lambda h, i: (0, h)),
          pl.BlockSpec((LANES, LANES), lambda h, i: (0, h)),
      ],
      out_specs=pl.BlockSpec((bt, LANES), lambda h, i: (i, h)),
      out_shape=jax.ShapeDtypeStruct((t, W_MIX), BF16),
      scratch_shapes=[pltpu.VMEM((LANES, LANES), F32), pltpu.VMEM((4, SUBLANES, LANES), F32)],
      compiler_params=pltpu.CompilerParams(
          dimension_semantics=("parallel", "arbitrary"), vmem_limit_bytes=VMEM_LIMIT),
      name="rwkv7_mixer",
  )(y, y, y, y, y, row(mu), row(mu), row(mu), row(mu), row(w0), row(a0), row(k_k), row(k_a),
    row(r_k), row(ln_g), row(ln_b), wup_p, aup_p)


def _gdn_kernel(q_ref, k_ref, v_ref, g_ref, z_ref, cq_ref, ck_ref, cv_ref, alog_ref, dtb_ref,
                ng_ref, o_ref, s_ref, prev_ref):
  bt = q_ref.shape[0]
  n = CHUNK
  h = pl.program_id(0)

  @pl.when(pl.program_id(1) == 0)
  def _():
    s_ref[...] = jnp.zeros_like(s_ref)
    prev_ref[...] = jnp.zeros_like(prev_ref)

  def l2n(x):
    return x * lax.rsqrt(jnp.sum(x * x, axis=-1, keepdims=True) + L2_EPS)

  q = l2n(_causal_conv_silu(q_ref, cq_ref, prev_ref, 0)) * (GD_HD ** -0.5)
  k = l2n(_causal_conv_silu(k_ref, ck_ref, prev_ref, 1))
  v = _causal_conv_silu(v_ref, cv_ref, prev_ref, 2)
  gates = g_ref[...]
  beta = _sigmoid(_select_lane(gates, h))
  log_a = -jnp.exp(alog_ref[...]) * _softplus(_select_lane(gates, GD_H + h) + dtb_ref[...])

  tr, tc = _tri(n)
  lower_incl = jnp.where(tc <= tr, 1.0, 0.0)
  upper_incl = jnp.where(tr <= tc, 1.0, 0.0)
  ones = jnp.ones((n, n), F32)

  s = s_ref[...]
  outs = []
  for ci in range(bt // n):
    sl = slice(ci * n, (ci + 1) * n)
    la = log_a[sl]
    g_col = _dot(lower_incl, la)
    g_row = _dot(ones, upper_incl * la[:, :n])
    diff = g_col[:, :n] - g_row
    decay = jnp.where(tc <= tr, jnp.exp(jnp.where(tc <= tr, diff, 0.0)), 0.0)
    kc, qc, bc = k[sl], q[sl], beta[sl]
    kb = kc * bc
    m = jnp.where(tc < tr, _dot_nt(kb, kc) * decay, 0.0)
    t_inv = _inv_unit_lower(m, 5)
    e_g = jnp.exp(g_col)
    u = _dot(t_inv, v[sl] * bc)
    w = _dot(t_inv, kb * e_g)
    v_new = u - _dot(w, s)
    attn = _dot_nt(qc, kc) * decay
    outs.append(_dot(qc * e_g, s) + _dot(attn, v_new))
    g_end = g_col[n - 1:n]
    s = jnp.exp(g_end) * s + _dot_tn(kc * jnp.exp(g_end - g_col), v_new)
  s_ref[...] = s
  o = jnp.concatenate(outs, axis=0)
  o_ref[...] = (_head_rms(o, ng_ref[...]) * _silu(z_ref[...])).astype(o_ref.dtype)


def gdn_mixer(y, conv_w, a_log, dt_bias, norm_g, *, bt):
  t = y.shape[0]
  cb = lambda name: _COL[name] // LANES
  col = lambda name: pl.BlockSpec((bt, LANES), lambda h, i, c=cb(name): (i, c + h))
  cw = lambda part: pl.BlockSpec((CONV_K, LANES), lambda h, i, p=part: (0, p * GD_H + h))
  scal = lambda: pl.BlockSpec((None, 1, LANES), lambda h, i: (h, 0, 0))
  bcast = lambda p: jnp.broadcast_to(p[:, None, None], (GD_H, 1, LANES))
  return pl.pallas_call(
      _gdn_kernel,
      grid=(GD_H, t // bt),
      in_specs=[
          col("D_q"), col("D_k"), col("D_v"),
          pl.BlockSpec((bt, LANES), lambda h, i, c=cb("D_g"): (i, c)),
          col("D_z"), cw(0), cw(1), cw(2), scal(), scal(),
          pl.BlockSpec((1, LANES), lambda h, i: (0, h)),
      ],
      out_specs=pl.BlockSpec((bt, LANES), lambda h, i: (i, h)),
      out_shape=jax.ShapeDtypeStruct((t, W_MIX), BF16),
      scratch_shapes=[pltpu.VMEM((GD_HD, GD_HD), F32), pltpu.VMEM((3, SUBLANES, LANES), F32)],
      compiler_params=pltpu.CompilerParams(
          dimension_semantics=("parallel", "arbitrary"), vmem_limit_bytes=VMEM_LIMIT),
      name="gdn_mixer",
  )(y, y, y, y, y, conv_w, conv_w, conv_w, bcast(a_log), bcast(dt_bias), norm_g.reshape(1, -1))


def _mlstm_kernel(q_ref, k_ref, v_ref, og_ref, z_ref, g_ref, cq_ref, ck_ref, ib_ref, fb_ref,
                  ng_ref, o_ref, c_ref, n_ref, m_ref, prev_ref):
  bt = q_ref.shape[0]
  n = CHUNK
  h = pl.program_id(0)

  @pl.when(pl.program_id(1) == 0)
  def _():
    c_ref[...] = jnp.zeros_like(c_ref)
    n_ref[...] = jnp.zeros_like(n_ref)
    m_ref[...] = jnp.zeros_like(m_ref)
    prev_ref[...] = jnp.zeros_like(prev_ref)

  q = _causal_conv_silu(q_ref, cq_ref, prev_ref, 0)
  k = _causal_conv_silu(k_ref, ck_ref, prev_ref, 1) * (ML_HD ** -0.5)
  v = v_ref[...]
  gates = g_ref[...]
  log_i = _select_lane(gates, h) + ib_ref[...]
  log_f = -_softplus(-(_select_lane(gates, ML_H + h) + fb_ref[...]))

  tr, tc = _tri(n)
  causal = tc <= tr
  lower_incl = jnp.where(causal, 1.0, 0.0)
  upper_incl = jnp.where(tr <= tc, 1.0, 0.0)
  eye = jnp.where(tr == tc, 1.0, 0.0)
  ones = jnp.ones((n, n), F32)

  c_st, n_st, m_st = c_ref[...], n_ref[...], m_ref[...]
  outs = []
  for ci in range(bt // n):
    sl = slice(ci * n, (ci + 1) * n)
    fc, ic = log_f[sl], log_i[sl]
    g_col = _dot(lower_incl, fc)
    g_row = _dot(ones, upper_incl * fc[:, :n])
    i_row = _dot(ones, eye * ic[:, :n])
    dmat = jnp.where(causal, g_col[:, :n] - g_row + i_row, -jnp.inf)
    inter = g_col + m_st
    m_row = jnp.maximum(jnp.max(dmat, axis=-1, keepdims=True), inter)
    qc, kc, vc = q[sl], k[sl], v[sl]
    sc = _dot_nt(qc, kc) * jnp.exp(dmat - m_row[:, :n])
    w_inter = jnp.exp(inter - m_row)[:, :1]
    num = _dot(sc, vc) + w_inter * _dot(qc, c_st)
    den = (jnp.sum(sc, axis=-1, keepdims=True)
           + w_inter * jnp.sum(qc * n_st, axis=-1, keepdims=True))
    outs.append(num / jnp.maximum(jnp.abs(den), jnp.exp(-m_row[:, :1])))
    g_end = g_col[n - 1:n]
    log_w = g_end - g_col + ic
    m_new = jnp.maximum(g_end + m_st, jnp.max(log_w, axis=0, keepdims=True))
    carry = jnp.exp(g_end + m_st - m_new)[:, :1]
    wk = jnp.exp(log_w - m_new)[:, :1] * kc
    c_st = carry * c_st + _dot_tn(wk, vc)
    n_st = carry * n_st + jnp.sum(wk, axis=0, keepdims=True)
    m_st = m_new
  c_ref[...] = c_st
  n_ref[...] = n_st
  m_ref[...] = m_st
  hh = _head_rms(jnp.concatenate(outs, axis=0), ng_ref[...]) * _sigmoid(og_ref[...])
  o_ref[...] = (hh * _silu(z_ref[...])).astype(o_ref.dtype)


def mlstm_mixer(y, conv_w, i_bias, f_bias, norm_g, *, bt):
  t = y.shape[0]
  cb = lambda name: _COL[name] // ML_HD
  col = lambda name: pl.BlockSpec((bt, ML_HD), lambda h, i, c=cb(name): (i, c + h))
  cw = lambda part: pl.BlockSpec((CONV_K, ML_HD), lambda h, i, p=part: (0, p * ML_H + h))
  scal = lambda: pl.BlockSpec((None, 1, LANES), lambda h, i: (h, 0, 0))
  bcast = lambda p: jnp.broadcast_to(p[:, None, None], (ML_H, 1, LANES))
  return pl.pallas_call(
      _mlstm_kernel,
      grid=(ML_H, t // bt),
      in_specs=[
          col("B_q"), col("B_k"), col("B_v"), col("B_o"), col("B_z"),
          pl.BlockSpec((bt, LANES), lambda h, i, c=_COL["B_g"] // LANES: (i, c)),
          cw(0), cw(1), scal(), scal(),
          pl.BlockSpec((1, ML_HD), lambda h, i: (0, h)),
      ],
      out_specs=pl.BlockSpec((bt, ML_HD), lambda h, i: (i, h)),
      out_shape=jax.ShapeDtypeStruct((t, W_MIX), BF16),
      scratch_shapes=[pltpu.VMEM((ML_HD, ML_HD), F32), pltpu.VMEM((1, ML_HD), F32),
                      pltpu.VMEM((1, LANES), F32), pltpu.VMEM((2, SUBLANES, ML_HD), F32)],
      compiler_params=pltpu.CompilerParams(
          dimension_semantics=("parallel", "arbitrary"), vmem_limit_bytes=VMEM_LIMIT),
      name="mlstm_mixer",
  )(y, y, y, y, y, y, conv_w, conv_w, bcast(i_bias), bcast(f_bias), norm_g.reshape(1, -1))


HG_SUB = 16


def _hgrn2_kernel(q_ref, f_ref, i_ref, z_ref, lb_ref, ng_ref, o_ref, st_ref):
  bt = q_ref.shape[0]
  n = CHUNK

  @pl.when(pl.program_id(1) == 0)
  def _():
    st_ref[...] = jnp.zeros_like(st_ref)

  lb = lb_ref[...]
  f = lb + (1.0 - lb) * _sigmoid(f_ref[...])
  q = _silu(q_ref[...])
  k = 1.0 - f
  log_f = jnp.log(f)
  v = i_ref[...]

  tr, tc = _tri(n)
  lower_incl = jnp.where(tc <= tr, 1.0, 0.0)
  sub_row = _iota((HG_SUB, HG_HD), 0)

  st = st_ref[...]
  outs = []
  for ci in range(bt // n):
    sl = slice(ci * n, (ci + 1) * n)
    lf, qc, kc, vc = log_f[sl], q[sl], k[sl], v[sl]
    b = _dot(lower_incl, lf)
    b_ex = b - lf
    o_inter = _dot_nt(qc * jnp.exp(b), st)
    rows = []
    for bi in range(n // HG_SUB):
      lo, hi = bi * HG_SUB, (bi + 1) * HG_SUB
      b_i, q_i, k_i, v_i = b[lo:hi], qc[lo:hi], kc[lo:hi], vc[lo:hi]
      acc = jnp.zeros((HG_SUB, HG_HD), F32)
      if bi > 0:
        b_ref = b_ex[lo:lo + 1]
        a_off = _dot_nt(q_i * jnp.exp(b_i - b_ref), kc[:lo] * jnp.exp(b_ref - b[:lo]))
        acc = acc + _dot(a_off, vc[:lo])
      for l in range(HG_SUB):
        live = sub_row >= l
        e = jnp.exp(jnp.where(live, b_i - b_i[l:l + 1], 0.0))
        a_col = jnp.sum(jnp.where(live, q_i * k_i[l:l + 1] * e, 0.0), axis=-1, keepdims=True)
        acc = acc + a_col * v_i[l:l + 1]
      rows.append(acc)
    outs.append(o_inter + jnp.concatenate(rows, axis=0))
    b_end = b[n - 1:n]
    st = st * jnp.exp(b_end) + _dot_tn(vc, kc * jnp.exp(b_end - b))
  st_ref[...] = st
  o = jnp.concatenate(outs, axis=0)
  o_ref[...] = (_head_rms(o, ng_ref[...]) * _silu(z_ref[...])).astype(o_ref.dtype)


def hgrn2_mixer(y, lower_bound, norm_g, *, bt):
  t = y.shape[0]
  cb = lambda name: _COL[name] // LANES
  col = lambda name: pl.BlockSpec((bt, LANES), lambda h, i, c=cb(name): (i, c + h))
  vec = lambda: pl.BlockSpec((1, LANES), lambda h, i: (0, h))
  return pl.pallas_call(
      _hgrn2_kernel,
      grid=(HG_H, t // bt),
      in_specs=[col("C_q"), col("C_f"), col("C_i"), col("C_z"), vec(), vec()],
      out_specs=pl.BlockSpec((bt, LANES), lambda h, i: (i, h)),
      out_shape=jax.ShapeDtypeStruct((t, W_MIX), BF16),
      scratch_shapes=[pltpu.VMEM((HG_HD, HG_HD), F32)],
      compiler_params=pltpu.CompilerParams(
          dimension_semantics=("parallel", "arbitrary"), vmem_limit_bytes=VMEM_LIMIT),
      name="hgrn2_mixer",
  )(y, y, y, y, lower_bound.reshape(1, -1), norm_g.reshape(1, -1))


def _lower_bounds_kernel(x_ref, o_ref):
  x = x_ref[...]
  e = jnp.exp(x - jnp.max(x, axis=0, keepdims=True))
  p = e / jnp.sum(e, axis=0, keepdims=True)
  acc = jnp.zeros_like(p[0:1])
  for l in range(DEPTH):
    o_ref[l:l + 1, :] = acc
    if l + 1 < DEPTH:
      acc = acc + p[l + 1:l + 2]


def hgrn2_lower_bounds(hg_lower):
  return pl.pallas_call(
      _lower_bounds_kernel,
      out_shape=jax.ShapeDtypeStruct(hg_lower.shape, F32),
      name="hgrn2_lower_bounds",
  )(hg_lower)


def _ada_kernel(c_ref, w_ref, b_ref, o_ref):
  ca = _silu(c_ref[...])
  tn = w_ref.shape[1]
  cols = [jnp.sum(w_ref[:, j * LANES:(j + 1) * LANES] * ca, axis=0, keepdims=True)
          for j in range(tn // LANES)]
  o_ref[...] = jnp.concatenate(cols, axis=1) + b_ref[...]


def adaln_params(c, w_ada, b_ada, *, tn=512):
  depth, d, n_out = w_ada.shape
  c_cols = jnp.broadcast_to(c.reshape(d, 1), (d, LANES))
  return pl.pallas_call(
      _ada_kernel,
      grid=(depth, n_out // tn),
      in_specs=[
          pl.BlockSpec((d, LANES), lambda l, j: (0, 0)),
          pl.BlockSpec((None, d, tn), lambda l, j: (l, 0, j)),
          pl.BlockSpec((None, 1, tn), lambda l, j: (l, 0, j)),
      ],
      out_specs=pl.BlockSpec((None, 1, tn), lambda l, j: (l, 0, j)),
      out_shape=jax.ShapeDtypeStruct((depth, 1, n_out), F32),
      compiler_params=pltpu.CompilerParams(
          dimension_semantics=("parallel", "parallel"), vmem_limit_bytes=VMEM_LIMIT),
      name="adaln_params",
  )(c_cols, w_ada, b_ada.reshape(depth, 1, n_out))


def _norm_mod_kernel(x_ref, g_ref, shift_ref, scale_ref, o_ref):
  x = x_ref[...]
  y = x * lax.rsqrt(jnp.mean(x * x, axis=-1, keepdims=True) + NORM_EPS) * g_ref[...]
  o_ref[...] = (y * (1.0 + scale_ref[...]) + shift_ref[...]).astype(o_ref.dtype)


def norm_modulate(x, g, shift, scale, *, tm=256):
  t, d = x.shape
  vec = pl.BlockSpec((1, d), lambda i: (0, 0))
  return pl.pallas_call(
      _norm_mod_kernel,
      grid=(t // tm,),
      in_specs=[pl.BlockSpec((tm, d), lambda i: (i, 0)), vec, vec, vec],
      out_specs=pl.BlockSpec((tm, d), lambda i: (i, 0)),
      out_shape=jax.ShapeDtypeStruct((t, d), BF16),
      compiler_params=pltpu.CompilerParams(
          dimension_semantics=("parallel",), vmem_limit_bytes=VMEM_LIMIT),
      name="norm_modulate",
  )(x, g.reshape(1, d), shift, scale)


def _final_norm_kernel(x_ref, g_ref, o_ref):
  x = x_ref[...]
  o_ref[...] = x * lax.rsqrt(jnp.mean(x * x, axis=-1, keepdims=True) + NORM_EPS) * g_ref[...]


def final_norm(x, g, *, tm=256):
  t, d = x.shape
  return pl.pallas_call(
      _final_norm_kernel,
      grid=(t // tm,),
      in_specs=[pl.BlockSpec((tm, d), lambda i: (i, 0)), pl.BlockSpec((1, d), lambda i: (0, 0))],
      out_specs=pl.BlockSpec((tm, d), lambda i: (i, 0)),
      out_shape=jax.ShapeDtypeStruct((t, d), F32),
      compiler_params=pltpu.CompilerParams(
          dimension_semantics=("parallel",), vmem_limit_bytes=VMEM_LIMIT),
      name="final_norm",
  )(x, g.reshape(1, d))


def _in_proj_kernel(h_ref, w_ref, o_ref):
  o_ref[...] = jnp.dot(h_ref[...], w_ref[...], preferred_element_type=F32)


def in_proj(h, w, *, tm=512, tn=1280):
  t, d = h.shape
  n_out = w.shape[1]
  return pl.pallas_call(
      _in_proj_kernel,
      grid=(n_out // tn, t // tm),
      in_specs=[pl.BlockSpec((tm, d), lambda j, i: (i, 0)),
                pl.BlockSpec((d, tn), lambda j, i: (0, j))],
      out_specs=pl.BlockSpec((tm, tn), lambda j, i: (i, j)),
      out_shape=jax.ShapeDtypeStruct((t, n_out), F32),
      compiler_params=pltpu.CompilerParams(
          dimension_semantics=("parallel", "parallel"), vmem_limit_bytes=VMEM_LIMIT),
      name="in_proj",
  )(h, w)


def _out_proj_kernel(oa_ref, ob_ref, oc_ref, od_ref, w_ref, x_ref, gate_ref, o_ref):
  acc = jnp.dot(oa_ref[...], w_ref[0:W_MIX, :], preferred_element_type=F32)
  for g, ref in enumerate((ob_ref, oc_ref, od_ref), start=1):
    acc = acc + jnp.dot(ref[...], w_ref[g * W_MIX:(g + 1) * W_MIX, :],
                        preferred_element_type=F32)
  o_ref[...] = x_ref[...] + gate_ref[...] * acc


def out_proj_residual(o_groups, w, x, gate, *, tm=512, tn=1024):
  t, d = x.shape
  og = pl.BlockSpec((tm, W_MIX), lambda j, i: (i, 0))
  return pl.pallas_call(
      _out_proj_kernel,
      grid=(d // tn, t // tm),
      in_specs=[og, og, og, og,
                pl.BlockSpec((4 * W_MIX, tn), lambda j, i: (0, j)),
                pl.BlockSpec((tm, tn), lambda j, i: (i, j)),
                pl.BlockSpec((1, tn), lambda j, i: (0, j))],
      out_specs=pl.BlockSpec((tm, tn), lambda j, i: (i, j)),
      out_shape=jax.ShapeDtypeStruct((t, d), F32),
      compiler_params=pltpu.CompilerParams(
          dimension_semantics=("parallel", "parallel"), vmem_limit_bytes=VMEM_LIMIT),
      name="out_proj_residual",
  )(*o_groups, w, x, gate)


MIXER_BT = 256


def kernel(x, c, norm_g, w_ada, b_ada, w_in, w_out, rw_mu, rw_w0, rw_w_up, rw_a0, rw_a_up, rw_k_k,
           rw_k_a, rw_r_k, rw_ln_g, rw_ln_b, ml_conv, ml_i_bias, ml_f_bias, ml_norm_g, hg_lower,
           hg_norm_g, gd_conv, gd_A_log, gd_dt_bias, gd_norm_g, final_g):
  bsz, seq, d = x.shape
  assert bsz == 1 and d == D_MODEL
  xs = x.reshape(seq, d)
  lower_bounds = hgrn2_lower_bounds(hg_lower)
  ada = adaln_params(c, w_ada, b_ada)
  bt = MIXER_BT
  for l in range(DEPTH):
    shift, scale, gate = ada[l, :, :d], ada[l, :, d:2 * d], ada[l, :, 2 * d:]
    h = norm_modulate(xs, norm_g[l], shift, scale)
    y = in_proj(h, relayout_columns(w_in[l]).astype(BF16))
    o_a = rwkv7_mixer(y, rw_mu[l], rw_w0[l], rw_w_up[l], rw_a0[l], rw_a_up[l], rw_k_k[l],
                      rw_k_a[l], rw_r_k[l], rw_ln_g[l], rw_ln_b[l], bt=bt)
    o_b = mlstm_mixer(y, ml_conv[l], ml_i_bias[l], ml_f_bias[l], ml_norm_g[l], bt=bt)
    o_c = hgrn2_mixer(y, lower_bounds[l], hg_norm_g[l], bt=bt)
    o_d = gdn_mixer(y, gd_conv[l], gd_A_log[l], gd_dt_bias[l], gd_norm_g[l], bt=bt)
    xs = out_proj_residual((o_a, o_b, o_c, o_d), w_out[l].astype(BF16), xs, gate)
  return final_norm(xs, final_g).reshape(bsz, seq, d)
```

```python
import functools

import jax
import jax.numpy as jnp
from jax import lax
from jax.experimental import pallas as pl
from jax.experimental.pallas import tpu as pltpu

F32 = jnp.float32
BF16 = jnp.bfloat16

D_MODEL = 4096
DEPTH = 4
W_MIX = 1024
RW_HEAD = 64
RW_LORA = 64
RW_DECAY_SCALE = 0.606531
RW_GN_EPS = 64e-5
ML_H = 4
ML_HD = 256
HG_H = 8
HG_HD = 128
GD_H = 8
GD_HD = 128
CONV_K = 4
CHUNK = 64
_LOG2_CHUNK = 6
_LOG2_RW_HEAD = 6
assert 1 << _LOG2_CHUNK == CHUNK and 1 << _LOG2_RW_HEAD == RW_HEAD
NORM_EPS = 1e-6
L2_EPS = 1e-6

LANES = 128
SUBLANES = 8
VMEM_LIMIT = 48 * 1024 * 1024

_COL = dict(
    B_q=0, B_k=1024, B_v=2048, B_o=3072, B_z=4096,
    C_q=5120, C_f=6144, C_i=7168, C_z=8192,
    A_r=9216, A_k=10240, A_v=11264, A_z=12288,
    D_q=13312, D_k=14336, D_v=15360, D_z=16384,
    A_lo=17408, B_g=17536, D_g=17664,
)
P_PAD = 17920
_A0, _B0, _C0, _D0 = 0, 4224, 9352, 13448
_SRC = (
    ("B_q", _B0, 1024), ("B_k", _B0 + 1024, 1024), ("B_v", _B0 + 2048, 1024),
    ("B_o", _B0 + 3072, 1024), ("B_z", _B0 + 4104, 1024),
    ("C_q", _C0, 1024), ("C_f", _C0 + 1024, 1024), ("C_i", _C0 + 2048, 1024),
    ("C_z", _C0 + 3072, 1024),
    ("A_r", _A0, 1024), ("A_k", _A0 + 1024, 1024), ("A_v", _A0 + 2048, 1024),
    ("A_z", _A0 + 3200, 1024),
    ("D_q", _D0, 1024), ("D_k", _D0 + 1024, 1024), ("D_v", _D0 + 2048, 1024),
    ("D_z", _D0 + 3088, 1024),
    ("A_lo", _A0 + 3072, 128), ("B_g", _B0 + 4096, 8), ("D_g", _D0 + 3072, 16),
)


def relayout_columns(w):
  parts, pos = [], 0
  for name, src, width in _SRC:
    assert _COL[name] == pos, (name, pos)
    parts.append(w[..., src:src + width])
    pad = (-width) % LANES
    if pad:
      parts.append(jnp.zeros(w.shape[:-1] + (pad,), w.dtype))
    pos += width + pad
  parts.append(jnp.zeros(w.shape[:-1] + (P_PAD - pos,), w.dtype))
  return jnp.concatenate(parts, axis=-1)


def _iota(shape, dim):
  return lax.broadcasted_iota(jnp.int32, shape, dim)


def _dot(a, b):
  return jnp.dot(a.astype(BF16), b.astype(BF16), preferred_element_type=F32)


def _dot_nt(a, b):
  return lax.dot_general(a.astype(BF16), b.astype(BF16), (((1,), (1,)), ((), ())),
                         preferred_element_type=F32)


def _dot_tn(a, b):
  return lax.dot_general(a.astype(BF16), b.astype(BF16), (((0,), (0,)), ((), ())),
                         preferred_element_type=F32)


def _bf16_terms(x, n_terms=3):
  terms, rest = [], x
  for _ in range(n_terms):
    t = rest.astype(BF16)
    terms.append(t)
    rest = rest - t.astype(F32)
  return terms


def _mask_dot(mask, x, n_terms=3):
  m = mask.astype(BF16)
  return functools.reduce(
      jnp.add, [jnp.dot(m, t, preferred_element_type=F32) for t in _bf16_terms(x, n_terms)])


def _dot_mask(x, mask, n_terms=3):
  m = mask.astype(BF16)
  return functools.reduce(
      jnp.add, [jnp.dot(t, m, preferred_element_type=F32) for t in _bf16_terms(x, n_terms)])


def _sigmoid(x):
  return 1.0 / (1.0 + jnp.exp(-x))


def _silu(x):
  return x * _sigmoid(x)


def _softplus(x):
  return jnp.maximum(x, 0.0) + jnp.log1p(jnp.exp(-jnp.abs(x)))


def _shift_rows(x, prev8, j):
  rolled = pltpu.roll(x, j, 0)
  top = jnp.where(_iota((SUBLANES, x.shape[1]), 0) < j, pltpu.roll(prev8, j, 0),
                  rolled[:SUBLANES])
  return jnp.concatenate([top, rolled[SUBLANES:]], axis=0)


def _causal_conv_silu(x_ref, w_ref, prev_ref, slot):
  x = x_ref[...]
  prev8 = prev_ref[slot]
  acc = x * w_ref[CONV_K - 1:CONV_K, :]
  for j in range(1, CONV_K):
    acc = acc + _shift_rows(x, prev8, j) * w_ref[CONV_K - 1 - j:CONV_K - j, :]
  prev_ref[slot] = x[x.shape[0] - SUBLANES:]
  return _silu(acc)


def _tri(n):
  r, c = _iota((n, n), 0), _iota((n, n), 1)
  return r, c


def _chunk_masks(rows):
  r, c = _tri(rows)
  same = (r >> _LOG2_CHUNK) == (c >> _LOG2_CHUNK)
  return same & (c < r), same & (c <= r)


def _inv_unit_lower(mats):
  n = mats[0].shape[0]
  r, c = _tri(n)
  same = lambda log2_size: (r >> log2_size) == (c >> log2_size)
  base = 3
  mm = lambda x, y: jnp.dot(x.astype(BF16), y.astype(BF16), preferred_element_type=F32)
  diag = [jnp.where(same(base), a, 0.0).astype(BF16) for a in mats]
  xs = [jnp.where(r == c, 1.0, 0.0) - d for d in diag]
  ps = [mm(d, d) for d in diag]
  xs = [x + mm(x, p) for x, p in zip(xs, ps)]
  ps = [mm(p, p) for p in ps]
  xs = [x + mm(x, p) for x, p in zip(xs, ps)]
  for level in range(base, _LOG2_CHUNK):
    sibling = same(level + 1) & jnp.logical_not(same(level))
    offs = [jnp.where(sibling, a, 0.0) for a in mats]
    half = [mm(x, o) for x, o in zip(xs, offs)]
    xs = [x - mm(h, x) for x, h in zip(xs, half)]
  return xs


def _select_lane(x, lane_idx):
  return jnp.sum(jnp.where(_iota(x.shape, 1) == lane_idx, x, 0.0), axis=-1, keepdims=True)


def _select_row(x, row_idx):
  return jnp.sum(jnp.where(_iota(x.shape, 0) == row_idx, x, 0.0), axis=0, keepdims=True)


def _rows_of_chunk(x, ci, total_rows):
  parts = []
  if ci:
    parts.append(jnp.zeros((ci * CHUNK, x.shape[1]), x.dtype))
  parts.append(x)
  rest = total_rows - (ci + 1) * CHUNK
  if rest:
    parts.append(jnp.zeros((rest, x.shape[1]), x.dtype))
  return jnp.concatenate(parts, axis=0)


def _head_rms(o, g):
  return o * lax.rsqrt(jnp.mean(o * o, axis=-1, keepdims=True) + NORM_EPS) * g


def _rwkv_kernel(r_ref, k_ref, v_ref, lo_ref, z_ref, mur_ref, muk_ref, muv_ref, mulo_ref,
                 w0_ref, a0_ref, kkw_ref, kaw_ref, rkw_ref, lng_ref, lnb_ref, wup_ref, aup_ref,
                 o_ref, s_ref, prev_ref, prevlo_ref):
  bt, width = r_ref.shape
  n = CHUNK
  n_pairs = width // LANES
  n_chunks = bt // n

  @pl.when(pl.program_id(1) == 0)
  def _():
    s_ref[...] = jnp.zeros_like(s_ref)
    prev_ref[...] = jnp.zeros_like(prev_ref)
    prevlo_ref[...] = jnp.zeros_like(prevlo_ref)

  def token_shift(x_ref, mu_ref, carry_ref, slot):
    x = x_ref[...]
    prev = _shift_rows(x, carry_ref[slot], 1)
    carry_ref[slot] = x[bt - SUBLANES:]
    return x + (prev - x) * mu_ref[...]

  r = token_shift(r_ref, mur_ref, prev_ref, 0)
  k = token_shift(k_ref, muk_ref, prev_ref, 1)
  v = token_shift(v_ref, muv_ref, prev_ref, 2)
  lo = token_shift(lo_ref, mulo_ref, prevlo_ref, 0)

  lo_act = jnp.where(_iota((bt, LANES), 1) < RW_LORA, jnp.tanh(lo), lo)
  log_w = -RW_DECAY_SCALE * _sigmoid(w0_ref[...] + _dot(lo_act, wup_ref[...]))
  a = _sigmoid(a0_ref[...] + _dot(lo_act, aup_ref[...]))

  hr, hc = _tri(width)
  head_ones = jnp.where(hr >> _LOG2_RW_HEAD == hc >> _LOG2_RW_HEAD, 1.0, 0.0)
  kkp = k * kkw_ref[...]
  k2 = k * (1.0 + (a - 1.0) * kaw_ref[...])
  sums = _dot(jnp.concatenate([kkp * kkp, r * k2 * rkw_ref[...]], axis=0), head_ones)
  kk = kkp * lax.rsqrt(sums[:bt] + L2_EPS)
  b = kk * a
  bonus = sums[bt:] * v

  strict, incl = _chunk_masks(bt)
  lc = _mask_dot(jnp.where(incl, 1.0, 0.0), log_w, 2)
  g_in = jnp.exp(lc)
  g_inv = jnp.exp(-lc)
  kap_all = kk * jnp.exp(lc - log_w)
  rt_all = r * g_in
  kh_all = k2 * g_inv
  bh_all = b * g_inv

  lane = _iota((bt, LANES), 1)
  lane_n = _iota((n, LANES), 1)
  head_masks = [jnp.where(lane < RW_HEAD, 1.0, 0.0), jnp.where(lane >= RW_HEAD, 1.0, 0.0)]
  chunk_head_masks = [jnp.where(lane_n < RW_HEAD, 1.0, 0.0), jnp.where(lane_n >= RW_HEAD, 1.0, 0.0)]
  pr, pc = _tri(LANES)
  pair_diag = (pr >> _LOG2_RW_HEAD) == (pc >> _LOG2_RW_HEAD)

  def pair(x, p):
    return x[:, p * LANES:(p + 1) * LANES]

  vheads = [(p, h) for p in range(n_pairs) for h in range(2)]
  big = [_dot_nt(jnp.concatenate([pair(kap_all, p), pair(rt_all, p)], axis=0),
                 jnp.concatenate([pair(kh_all, p) * head_masks[h],
                                  pair(bh_all, p) * head_masks[h]], axis=0))
         for p, h in vheads]
  a_kk = [jnp.where(strict, m[:bt, :bt], 0.0) for m in big]
  a_bk = [jnp.where(strict, m[:bt, bt:], 0.0) for m in big]
  a_kr = [jnp.where(incl, m[bt:, :bt], 0.0) for m in big]
  a_br = [jnp.where(incl, m[bt:, bt:], 0.0) for m in big]
  t_inv = _inv_unit_lower(a_bk)
  av = [_dot(jnp.concatenate([a_kk[i], a_kr[i]], axis=0), pair(v, p) * head_masks[h])
        for i, (p, h) in enumerate(vheads)]
  xy = [_dot(t_inv[i], jnp.concatenate([pair(kap_all, p) * head_masks[h], av[i][:bt]], axis=1))
        for i, (p, h) in enumerate(vheads)]
  x_p = [xy[2 * p][:, :LANES] + xy[2 * p + 1][:, :LANES] for p in range(n_pairs)]
  y_p = [xy[2 * p][:, LANES:] + xy[2 * p + 1][:, LANES:] for p in range(n_pairs)]
  akrv_p = [av[2 * p][bt:] + av[2 * p + 1][bt:] for p in range(n_pairs)]

  s = [s_ref[p] for p in range(n_pairs)]
  outs = [[] for _ in range(n_pairs)]
  for ci in range(n_chunks):
    sl = slice(ci * n, (ci + 1) * n)
    g_last = [pair(g_in, p)[(ci + 1) * n - 1:(ci + 1) * n] for p in range(n_pairs)]
    reads = [_dot_nt(jnp.concatenate([x_p[p][sl], pair(rt_all, p)[sl]], axis=0), s[p])
             for p in range(n_pairs)]
    u = [reads[p][:n] + y_p[p][sl] for p in range(n_pairs)]
    for p in range(n_pairs):
      u_rows = jnp.concatenate(
          [_rows_of_chunk(u[p] * chunk_head_masks[h], ci, bt) for h in range(2)], axis=0)
      a_br_rows = jnp.concatenate([a_br[2 * p][sl], a_br[2 * p + 1][sl]], axis=1)
      outs[p].append(reads[p][n:] + akrv_p[p][sl] - _dot(a_br_rows, u_rows))
      delta = _dot_tn(jnp.concatenate([pair(v, p)[sl], u[p]], axis=0),
                      jnp.concatenate([pair(kh_all, p)[sl] * g_last[p],
                                       -pair(bh_all, p)[sl] * g_last[p]], axis=0))
      s[p] = s[p] * g_last[p] + jnp.where(pair_diag, delta, 0.0)
  for p in range(n_pairs):
    s_ref[p] = s[p]
  o = jnp.concatenate([jnp.concatenate(outs[p], axis=0) for p in range(n_pairs)], axis=1)

  inv_n = 1.0 / RW_HEAD
  mean = _dot(o, head_ones) * inv_n
  d = o - mean
  var = _dot(d * d, head_ones) * inv_n
  o = d * lax.rsqrt(var + RW_GN_EPS) * lng_ref[...] + lnb_ref[...] + bonus
  o_ref[...] = (o * _silu(z_ref[...])).astype(o_ref.dtype)


RW_PAIRS_PER_STEP = 4


def rwkv7_mixer(y, mu, w0, w_up, a0, a_up, k_k, k_a, r_k, ln_g, ln_b, *, bt):
  t = y.shape[0]
  width = RW_PAIRS_PER_STEP * LANES
  groups = W_MIX // width
  col = lambda name: pl.BlockSpec((bt, width), lambda g, i, c=_COL[name] // width: (i, c + g))
  vec = lambda: pl.BlockSpec((1, width), lambda g, i: (0, g))
  zeros = jnp.zeros((RW_LORA, W_MIX), F32)
  wup_p = jnp.concatenate([w_up, zeros], axis=0)
  aup_p = jnp.concatenate([zeros, a_up], axis=0)
  row = lambda p: p.reshape(1, -1)
  return pl.pallas_call(
      _rwkv_kernel,
      grid=(groups, t // bt),
      in_specs=[
          col("A_r"), col("A_k"), col("A_v"),
          pl.BlockSpec((bt, LANES), lambda g, i, c=_COL["A_lo"] // LANES: (i, c)),
          col("A_z"),
          vec(), vec(), vec(), pl.BlockSpec((1, LANES), lambda g, i: (0, 0)),
          vec(), vec(), vec(), vec(), vec(), vec(), vec(),
          pl.BlockSpec((LANES, width), lambda g, i: (0, g)),
          pl.BlockSpec((LANES, width), lambda g, i: (0, g)),
      ],
      out_specs=pl.BlockSpec((bt, width), lambda g, i: (i, g)),
      out_shape=jax.ShapeDtypeStruct((t, W_MIX), BF16),
      scratch_shapes=[pltpu.VMEM((RW_PAIRS_PER_STEP, LANES, LANES), F32),
                      pltpu.VMEM((3, SUBLANES, width), F32),
                      pltpu.VMEM((1, SUBLANES, LANES), F32)],
      compiler_params=pltpu.CompilerParams(
          dimension_semantics=("parallel", "arbitrary"), vmem_limit_bytes=VMEM_LIMIT),
      name="rwkv7_mixer",
  )(y, y, y, y, y, row(mu[:W_MIX]), row(mu[W_MIX:2 * W_MIX]), row(mu[2 * W_MIX:3 * W_MIX]),
    row(mu[3 * W_MIX:]), row(w0), row(a0), row(k_k), row(k_a), row(r_k), row(ln_g), row(ln_b),
    wup_p, aup_p)


def _gdn_kernel(q_ref, k_ref, v_ref, g_ref, z_ref, cq_ref, ck_ref, cv_ref, alog_ref, dtb_ref,
                ng_ref, o_ref, s_ref, prev_ref):
  bt, width = q_ref.shape
  n = CHUNK
  n_heads = width // LANES
  n_chunks = bt // n
  first_head = pl.program_id(0) * n_heads

  @pl.when(pl.program_id(1) == 0)
  def _():
    s_ref[...] = jnp.zeros_like(s_ref)
    prev_ref[...] = jnp.zeros_like(prev_ref)

  def l2n(x):
    return x * lax.rsqrt(jnp.sum(x * x, axis=-1, keepdims=True) + L2_EPS)

  def head(x, j):
    return x[:, j * LANES:(j + 1) * LANES]

  q_all = _causal_conv_silu(q_ref, cq_ref, prev_ref, 0)
  k_all = _causal_conv_silu(k_ref, ck_ref, prev_ref, 1)
  v_all = _causal_conv_silu(v_ref, cv_ref, prev_ref, 2)
  gates = g_ref[...]
  beta_all = _sigmoid(gates)
  log_a_all = -jnp.exp(alog_ref[...]) * _softplus(gates + dtb_ref[...])
  strict, incl = _chunk_masks(bt)
  g_all = _mask_dot(jnp.where(incl, 1.0, 0.0), log_a_all, 2)
  g_all_t = g_all.T

  qs, ks, vs, kbs, g_cols, decays = [], [], [], [], [], []
  for j in range(n_heads):
    beta = _select_lane(beta_all, first_head + j)
    g_col = _select_lane(g_all, GD_H + first_head + j)
    g_row = _select_row(g_all_t, GD_H + first_head + j)
    decays.append(jnp.exp(jnp.minimum(g_col - g_row, 0.0)))
    k = l2n(head(k_all, j))
    qs.append(l2n(head(q_all, j)) * (GD_HD ** -0.5))
    ks.append(k)
    kbs.append(k * beta)
    vs.append(head(v_all, j) * beta)
    g_cols.append(g_col)
  heads = range(n_heads)
  big = [_dot_nt(jnp.concatenate([kbs[j], qs[j]], axis=0), ks[j]) for j in heads]
  m = [jnp.where(strict, big[j][:bt] * decays[j], 0.0) for j in heads]
  attn = [jnp.where(incl, big[j][bt:] * decays[j], 0.0) for j in heads]
  t_inv = _inv_unit_lower(m)
  e_g = [jnp.exp(g) for g in g_cols]
  yx = [_dot(t_inv[j], jnp.concatenate([vs[j], kbs[j] * e_g[j]], axis=1)) for j in heads]
  qe = [qs[j] * e_g[j] for j in heads]

  s = [s_ref[j] for j in heads]
  outs = [[] for _ in heads]
  for ci in range(n_chunks):
    sl = slice(ci * n, (ci + 1) * n)
    reads = [_dot(jnp.concatenate([yx[j][sl, LANES:], qe[j][sl]], axis=0), s[j]) for j in heads]
    for j in heads:
      v_new = yx[j][sl, :LANES] - reads[j][:n]
      outs[j].append(reads[j][n:] + _dot(attn[j][sl], _rows_of_chunk(v_new, ci, bt)))
      g_end = g_cols[j][(ci + 1) * n - 1:(ci + 1) * n]
      s[j] = jnp.exp(g_end) * s[j] + _dot_tn(ks[j][sl] * jnp.exp(g_end - g_cols[j][sl]), v_new)
  for j in heads:
    s_ref[j] = s[j]
  o = jnp.concatenate(
      [_head_rms(jnp.concatenate(outs[j], axis=0), head(ng_ref[...], j)) for j in heads], axis=1)
  o_ref[...] = (o * _silu(z_ref[...])).astype(o_ref.dtype)


GD_HEADS_PER_STEP = 4


def gdn_mixer(y, conv_w, a_log, dt_bias, norm_g, *, bt):
  t = y.shape[0]
  width = GD_HEADS_PER_STEP * LANES
  groups = W_MIX // width
  col = lambda name: pl.BlockSpec((bt, width), lambda g, i, c=_COL[name] // width: (i, c + g))
  cw = lambda part: pl.BlockSpec((CONV_K, width), lambda g, i, p=part: (0, p * groups + g))
  fix = pl.BlockSpec((1, LANES), lambda g, i: (0, 0))
  on_gate_lanes = lambda p: jnp.zeros((1, LANES), F32).at[0, GD_H:2 * GD_H].set(p)
  return pl.pallas_call(
      _gdn_kernel,
      grid=(groups, t // bt),
      in_specs=[
          col("D_q"), col("D_k"), col("D_v"),
          pl.BlockSpec((bt, LANES), lambda g, i, c=_COL["D_g"] // LANES: (i, c)),
          col("D_z"), cw(0), cw(1), cw(2), fix, fix,
          pl.BlockSpec((1, width), lambda g, i: (0, g)),
      ],
      out_specs=pl.BlockSpec((bt, width), lambda g, i: (i, g)),
      out_shape=jax.ShapeDtypeStruct((t, W_MIX), BF16),
      scratch_shapes=[pltpu.VMEM((GD_HEADS_PER_STEP, GD_HD, GD_HD), F32),
                      pltpu.VMEM((3, SUBLANES, width), F32)],
      compiler_params=pltpu.CompilerParams(
          dimension_semantics=("parallel", "arbitrary"), vmem_limit_bytes=VMEM_LIMIT),
      name="gdn_mixer",
  )(y, y, y, y, y, conv_w, conv_w, conv_w, on_gate_lanes(a_log), on_gate_lanes(dt_bias),
    norm_g.reshape(1, -1))


def _mlstm_kernel(q_ref, k_ref, v_ref, og_ref, z_ref, g_ref, cq_ref, ck_ref, ib_ref, fb_ref,
                  ng_ref, o_ref, c_ref, n_ref, m_ref, prev_ref):
  bt = q_ref.shape[0]
  n = CHUNK
  h = pl.program_id(0)

  @pl.when(pl.program_id(1) == 0)
  def _():
    c_ref[...] = jnp.zeros_like(c_ref)
    n_ref[...] = jnp.zeros_like(n_ref)
    m_ref[...] = jnp.zeros_like(m_ref)
    prev_ref[...] = jnp.zeros_like(prev_ref)

  q = _causal_conv_silu(q_ref, cq_ref, prev_ref, 0)
  k = _causal_conv_silu(k_ref, ck_ref, prev_ref, 1) * (ML_HD ** -0.5)
  v = v_ref[...]
  gates = g_ref[...]
  log_i = _select_lane(gates, h) + ib_ref[...]
  log_f = -_softplus(-(_select_lane(gates, ML_H + h) + fb_ref[...]))

  tr, tc = _tri(n)
  causal = tc <= tr
  lower_incl = jnp.where(causal, 1.0, 0.0)
  upper_incl = jnp.where(tr <= tc, 1.0, 0.0)
  eye = jnp.where(tr == tc, 1.0, 0.0)
  ones = jnp.ones((n, n), F32)

  c_st, n_st, m_st = c_ref[...], n_ref[...], m_ref[...]
  outs = []
  for ci in range(bt // n):
    sl = slice(ci * n, (ci + 1) * n)
    fc, ic = log_f[sl], log_i[sl]
    g_col = _mask_dot(lower_incl, fc)
    g_row = _mask_dot(ones, upper_incl * fc[:, :n])
    i_row = _mask_dot(ones, eye * ic[:, :n])
    dmat = jnp.where(causal, g_col[:, :n] - g_row + i_row, -jnp.inf)
    inter = g_col + m_st
    m_row = jnp.maximum(jnp.max(dmat, axis=-1, keepdims=True), inter)
    qc, kc, vc = q[sl], k[sl], v[sl]
    sc = _dot_nt(qc, kc) * jnp.exp(dmat - m_row[:, :n])
    w_inter = jnp.exp(inter - m_row)[:, :1]
    num = _dot(sc, vc) + w_inter * _dot(qc, c_st)
    den = (jnp.sum(sc, axis=-1, keepdims=True)
           + w_inter * jnp.sum(qc * n_st, axis=-1, keepdims=True))
    outs.append(num / jnp.maximum(jnp.abs(den), jnp.exp(-m_row[:, :1])))
    g_end = g_col[n - 1:n]
    log_w = g_end - g_col + ic
    m_new = jnp.maximum(g_end + m_st, jnp.max(log_w, axis=0, keepdims=True))
    carry = jnp.exp(g_end + m_st - m_new)[:, :1]
    wk = jnp.exp(log_w - m_new)[:, :1] * kc
    c_st = carry * c_st + _dot_tn(wk, vc)
    n_st = carry * n_st + jnp.sum(wk, axis=0, keepdims=True)
    m_st = m_new
  c_ref[...] = c_st
  n_ref[...] = n_st
  m_ref[...] = m_st
  hh = _head_rms(jnp.concatenate(outs, axis=0), ng_ref[...]) * _sigmoid(og_ref[...])
  o_ref[...] = (hh * _silu(z_ref[...])).astype(o_ref.dtype)


def mlstm_mixer(y, conv_w, i_bias, f_bias, norm_g, *, bt):
  t = y.shape[0]
  cb = lambda name: _COL[name] // ML_HD
  col = lambda name: pl.BlockSpec((bt, ML_HD), lambda h, i, c=cb(name): (i, c + h))
  cw = lambda part: pl.BlockSpec((CONV_K, ML_HD), lambda h, i, p=part: (0, p * ML_H + h))
  scal = lambda: pl.BlockSpec((None, 1, LANES), lambda h, i: (h, 0, 0))
  bcast = lambda p: jnp.broadcast_to(p[:, None, None], (ML_H, 1, LANES))
  return pl.pallas_call(
      _mlstm_kernel,
      grid=(ML_H, t // bt),
      in_specs=[
          col("B_q"), col("B_k"), col("B_v"), col("B_o"), col("B_z"),
          pl.BlockSpec((bt, LANES), lambda h, i, c=_COL["B_g"] // LANES: (i, c)),
          cw(0), cw(1), scal(), scal(),
          pl.BlockSpec((1, ML_HD), lambda h, i: (0, h)),
      ],
      out_specs=pl.BlockSpec((bt, ML_HD), lambda h, i: (i, h)),
      out_shape=jax.ShapeDtypeStruct((t, W_MIX), BF16),
      scratch_shapes=[pltpu.VMEM((ML_HD, ML_HD), F32), pltpu.VMEM((1, ML_HD), F32),
                      pltpu.VMEM((1, LANES), F32), pltpu.VMEM((2, SUBLANES, ML_HD), F32)],
      compiler_params=pltpu.CompilerParams(
          dimension_semantics=("parallel", "arbitrary"), vmem_limit_bytes=VMEM_LIMIT),
      name="mlstm_mixer",
  )(y, y, y, y, y, y, conv_w, conv_w, bcast(i_bias), bcast(f_bias), norm_g.reshape(1, -1))


HG_SUB = 16


def _hgrn2_kernel(q_ref, f_ref, i_ref, z_ref, lb_ref, ng_ref, o_ref, st_ref):
  bt = q_ref.shape[0]
  n = CHUNK

  @pl.when(pl.program_id(1) == 0)
  def _():
    st_ref[...] = jnp.zeros_like(st_ref)

  lb = lb_ref[...]
  f = lb + (1.0 - lb) * _sigmoid(f_ref[...])
  q = _silu(q_ref[...])
  k = 1.0 - f
  log_f = jnp.log(f)
  v = i_ref[...]

  tr, tc = _tri(n)
  lower_incl = jnp.where(tc <= tr, 1.0, 0.0)
  sub_row = _iota((HG_SUB, HG_HD), 0)

  st = st_ref[...]
  outs = []
  for ci in range(bt // n):
    sl = slice(ci * n, (ci + 1) * n)
    lf, qc, kc, vc = log_f[sl], q[sl], k[sl], v[sl]
    b = _mask_dot(lower_incl, lf)
    b_ex = b - lf
    o_inter = _dot_nt(qc * jnp.exp(b), st)
    rows = []
    for bi in range(n // HG_SUB):
      lo, hi = bi * HG_SUB, (bi + 1) * HG_SUB
      b_i, q_i, k_i, v_i = b[lo:hi], qc[lo:hi], kc[lo:hi], vc[lo:hi]
      acc = jnp.zeros((HG_SUB, HG_HD), F32)
      if bi > 0:
        b_ref = b_ex[lo:lo + 1]
        a_off = _dot_nt(q_i * jnp.exp(b_i - b_ref), kc[:lo] * jnp.exp(b_ref - b[:lo]))
        acc = acc + _dot(a_off, vc[:lo])
      for l in range(HG_SUB):
        live = sub_row >= l
        e = jnp.exp(jnp.where(live, b_i - b_i[l:l + 1], 0.0))
        a_col = jnp.sum(jnp.where(live, q_i * k_i[l:l + 1] * e, 0.0), axis=-1, keepdims=True)
        acc = acc + a_col * v_i[l:l + 1]
      rows.append(acc)
    outs.append(o_inter + jnp.concatenate(rows, axis=0))
    b_end = b[n - 1:n]
    st = st * jnp.exp(b_end) + _dot_tn(vc, kc * jnp.exp(b_end - b))
  st_ref[...] = st
  o = jnp.concatenate(outs, axis=0)
  o_ref[...] = (_head_rms(o, ng_ref[...]) * _silu(z_ref[...])).astype(o_ref.dtype)


def hgrn2_mixer(y, lower_bound, norm_g, *, bt):
  t = y.shape[0]
  cb = lambda name: _COL[name] // LANES
  col = lambda name: pl.BlockSpec((bt, LANES), lambda h, i, c=cb(name): (i, c + h))
  vec = lambda: pl.BlockSpec((1, LANES), lambda h, i: (0, h))
  return pl.pallas_call(
      _hgrn2_kernel,
      grid=(HG_H, t // bt),
      in_specs=[col("C_q"), col("C_f"), col("C_i"), col("C_z"), vec(), vec()],
      out_specs=pl.BlockSpec((bt, LANES), lambda h, i: (i, h)),
      out_shape=jax.ShapeDtypeStruct((t, W_MIX), BF16),
      scratch_shapes=[pltpu.VMEM((HG_HD, HG_HD), F32)],
      compiler_params=pltpu.CompilerParams(
          dimension_semantics=("parallel", "arbitrary"), vmem_limit_bytes=VMEM_LIMIT),
      name="hgrn2_mixer",
  )(y, y, y, y, lower_bound.reshape(1, -1), norm_g.reshape(1, -1))


def _lower_bounds_kernel(x_ref, o_ref):
  x = x_ref[...]
  e = jnp.exp(x - jnp.max(x, axis=0, keepdims=True))
  p = e / jnp.sum(e, axis=0, keepdims=True)
  acc = jnp.zeros_like(p[0:1])
  for l in range(DEPTH):
    o_ref[l:l + 1, :] = acc
    if l + 1 < DEPTH:
      acc = acc + p[l + 1:l + 2]


def hgrn2_lower_bounds(hg_lower):
  return pl.pallas_call(
      _lower_bounds_kernel,
      out_shape=jax.ShapeDtypeStruct(hg_lower.shape, F32),
      name="hgrn2_lower_bounds",
  )(hg_lower)


def _ada_kernel(c_ref, w_ref, b_ref, o_ref):
  ca = _silu(c_ref[...])
  tn = w_ref.shape[1]
  cols = [jnp.sum(w_ref[:, j * LANES:(j + 1) * LANES] * ca, axis=0, keepdims=True)
          for j in range(tn // LANES)]
  o_ref[...] = jnp.concatenate(cols, axis=1) + b_ref[...]


def adaln_params(c, w_ada, b_ada, *, tn=512):
  depth, d, n_out = w_ada.shape
  c_cols = jnp.broadcast_to(c.reshape(d, 1), (d, LANES))
  return pl.pallas_call(
      _ada_kernel,
      grid=(depth, n_out // tn),
      in_specs=[
          pl.BlockSpec((d, LANES), lambda l, j: (0, 0)),
          pl.BlockSpec((None, d, tn), lambda l, j: (l, 0, j)),
          pl.BlockSpec((None, 1, tn), lambda l, j: (l, 0, j)),
      ],
      out_specs=pl.BlockSpec((None, 1, tn), lambda l, j: (l, 0, j)),
      out_shape=jax.ShapeDtypeStruct((depth, 1, n_out), F32),
      compiler_params=pltpu.CompilerParams(
          dimension_semantics=("parallel", "parallel"), vmem_limit_bytes=VMEM_LIMIT),
      name="adaln_params",
  )(c_cols, w_ada, b_ada.reshape(depth, 1, n_out))


def _norm_mod_kernel(x_ref, g_ref, shift_ref, scale_ref, o_ref):
  x = x_ref[...]
  y = x * lax.rsqrt(jnp.mean(x * x, axis=-1, keepdims=True) + NORM_EPS) * g_ref[...]
  o_ref[...] = (y * (1.0 + scale_ref[...]) + shift_ref[...]).astype(o_ref.dtype)


def norm_modulate(x, g, shift, scale, *, tm=256):
  t, d = x.shape
  vec = pl.BlockSpec((1, d), lambda i: (0, 0))
  return pl.pallas_call(
      _norm_mod_kernel,
      grid=(t // tm,),
      in_specs=[pl.BlockSpec((tm, d), lambda i: (i, 0)), vec, vec, vec],
      out_specs=pl.BlockSpec((tm, d), lambda i: (i, 0)),
      out_shape=jax.ShapeDtypeStruct((t, d), BF16),
      compiler_params=pltpu.CompilerParams(
          dimension_semantics=("parallel",), vmem_limit_bytes=VMEM_LIMIT),
      name="norm_modulate",
  )(x, g.reshape(1, d), shift, scale)


def _final_norm_kernel(x_ref, g_ref, o_ref):
  x = x_ref[...]
  o_ref[...] = x * lax.rsqrt(jnp.mean(x * x, axis=-1, keepdims=True) + NORM_EPS) * g_ref[...]


def final_norm(x, g, *, tm=256):
  t, d = x.shape
  return pl.pallas_call(
      _final_norm_kernel,
      grid=(t // tm,),
      in_specs=[pl.BlockSpec((tm, d), lambda i: (i, 0)), pl.BlockSpec((1, d), lambda i: (0, 0))],
      out_specs=pl.BlockSpec((tm, d), lambda i: (i, 0)),
      out_shape=jax.ShapeDtypeStruct((t, d), F32),
      compiler_params=pltpu.CompilerParams(
          dimension_semantics=("parallel",), vmem_limit_bytes=VMEM_LIMIT),
      name="final_norm",
  )(x, g.reshape(1, d))


def _in_proj_kernel(h_ref, w_ref, o_ref):
  o_ref[...] = jnp.dot(h_ref[...], w_ref[...], preferred_element_type=F32)


def in_proj(h, w, *, tm=512, tn=1280):
  t, d = h.shape
  n_out = w.shape[1]
  return pl.pallas_call(
      _in_proj_kernel,
      grid=(n_out // tn, t // tm),
      in_specs=[pl.BlockSpec((tm, d), lambda j, i: (i, 0)),
                pl.BlockSpec((d, tn), lambda j, i: (0, j))],
      out_specs=pl.BlockSpec((tm, tn), lambda j, i: (i, j)),
      out_shape=jax.ShapeDtypeStruct((t, n_out), F32),
      compiler_params=pltpu.CompilerParams(
          dimension_semantics=("parallel", "parallel"), vmem_limit_bytes=VMEM_LIMIT),
      name="in_proj",
  )(h, w)


def _out_proj_kernel(oa_ref, ob_ref, oc_ref, od_ref, w_ref, x_ref, gate_ref, o_ref):
  acc = jnp.dot(oa_ref[...], w_ref[0:W_MIX, :], preferred_element_type=F32)
  for g, ref in enumerate((ob_ref, oc_ref, od_ref), start=1):
    acc = acc + jnp.dot(ref[...], w_ref[g * W_MIX:(g + 1) * W_MIX, :],
                        preferred_element_type=F32)
  o_ref[...] = x_ref[...] + gate_ref[...] * acc


def out_proj_residual(o_groups, w, x, gate, *, tm=512, tn=1024):
  t, d = x.shape
  og = pl.BlockSpec((tm, W_MIX), lambda j, i: (i, 0))
  return pl.pallas_call(
      _out_proj_kernel,
      grid=(d // tn, t // tm),
      in_specs=[og, og, og, og,
                pl.BlockSpec((4 * W_MIX, tn), lambda j, i: (0, j)),
                pl.BlockSpec((tm, tn), lambda j, i: (i, j)),
                pl.BlockSpec((1, tn), lambda j, i: (0, j))],
      out_specs=pl.BlockSpec((tm, tn), lambda j, i: (i, j)),
      out_shape=jax.ShapeDtypeStruct((t, d), F32),
      compiler_params=pltpu.CompilerParams(
          dimension_semantics=("parallel", "parallel"), vmem_limit_bytes=VMEM_LIMIT),
      name="out_proj_residual",
  )(*o_groups, w, x, gate)


MIXER_BT = 256


def kernel(x, c, norm_g, w_ada, b_ada, w_in, w_out, rw_mu, rw_w0, rw_w_up, rw_a0, rw_a_up, rw_k_k,
           rw_k_a, rw_r_k, rw_ln_g, rw_ln_b, ml_conv, ml_i_bias, ml_f_bias, ml_norm_g, hg_lower,
           hg_norm_g, gd_conv, gd_A_log, gd_dt_bias, gd_norm_g, final_g):
  bsz, seq, d = x.shape
  assert bsz == 1 and d == D_MODEL
  xs = x.reshape(seq, d)
  lower_bounds = hgrn2_lower_bounds(hg_lower)
  ada = adaln_params(c, w_ada, b_ada)
  bt = MIXER_BT
  for l in range(DEPTH):
    shift, scale, gate = ada[l, :, :d], ada[l, :, d:2 * d], ada[l, :, 2 * d:]
    h = norm_modulate(xs, norm_g[l], shift, scale)
    y = in_proj(h, relayout_columns(w_in[l]).astype(BF16))
    o_a = rwkv7_mixer(y, rw_mu[l], rw_w0[l], rw_w_up[l], rw_a0[l], rw_a_up[l], rw_k_k[l],
                      rw_k_a[l], rw_r_k[l], rw_ln_g[l], rw_ln_b[l], bt=bt)
    o_b = mlstm_mixer(y, ml_conv[l], ml_i_bias[l], ml_f_bias[l], ml_norm_g[l], bt=bt)
    o_c = hgrn2_mixer(y, lower_bounds[l], hg_norm_g[l], bt=bt)
    o_d = gdn_mixer(y, gd_conv[l], gd_A_log[l], gd_dt_bias[l], gd_norm_g[l], bt=bt)
    xs = out_proj_residual((o_a, o_b, o_c, o_d), w_out[l].astype(BF16), xs, gate)
  return final_norm(xs, final_g).reshape(bsz, seq, d)
```

```python
import functools

import jax
import jax.numpy as jnp
from jax import lax
from jax.experimental import pallas as pl
from jax.experimental.pallas import tpu as pltpu

F32 = jnp.float32
BF16 = jnp.bfloat16

D_MODEL = 4096
DEPTH = 4
W_MIX = 1024
RW_HEAD = 64
RW_LORA = 64
RW_DECAY_SCALE = 0.606531
RW_GN_EPS = 64e-5
ML_H = 4
ML_HD = 256
HG_H = 8
HG_HD = 128
GD_H = 8
GD_HD = 128
CONV_K = 4
CHUNK = 64
_LOG2_CHUNK = 6
_LOG2_RW_HEAD = 6
assert 1 << _LOG2_CHUNK == CHUNK and 1 << _LOG2_RW_HEAD == RW_HEAD
NORM_EPS = 1e-6
L2_EPS = 1e-6

LANES = 128
SUBLANES = 8
VMEM_LIMIT = 48 * 1024 * 1024

_COL = dict(
    B_q=0, B_k=1024, B_v=2048, B_o=3072, B_z=4096,
    C_q=5120, C_f=6144, C_i=7168, C_z=8192,
    A_r=9216, A_k=10240, A_v=11264, A_z=12288,
    D_q=13312, D_k=14336, D_v=15360, D_z=16384,
    A_lo=17408, B_g=17536, D_g=17664,
)
P_PAD = 17920
_A0, _B0, _C0, _D0 = 0, 4224, 9352, 13448
_SRC = (
    ("B_q", _B0, 1024), ("B_k", _B0 + 1024, 1024), ("B_v", _B0 + 2048, 1024),
    ("B_o", _B0 + 3072, 1024), ("B_z", _B0 + 4104, 1024),
    ("C_q", _C0, 1024), ("C_f", _C0 + 1024, 1024), ("C_i", _C0 + 2048, 1024),
    ("C_z", _C0 + 3072, 1024),
    ("A_r", _A0, 1024), ("A_k", _A0 + 1024, 1024), ("A_v", _A0 + 2048, 1024),
    ("A_z", _A0 + 3200, 1024),
    ("D_q", _D0, 1024), ("D_k", _D0 + 1024, 1024), ("D_v", _D0 + 2048, 1024),
    ("D_z", _D0 + 3088, 1024),
    ("A_lo", _A0 + 3072, 128), ("B_g", _B0 + 4096, 8), ("D_g", _D0 + 3072, 16),
)


def relayout_columns(w):
  parts, pos = [], 0
  for name, src, width in _SRC:
    assert _COL[name] == pos, (name, pos)
    parts.append(w[..., src:src + width])
    pad = (-width) % LANES
    if pad:
      parts.append(jnp.zeros(w.shape[:-1] + (pad,), w.dtype))
    pos += width + pad
  parts.append(jnp.zeros(w.shape[:-1] + (P_PAD - pos,), w.dtype))
  return jnp.concatenate(parts, axis=-1)


RELAY_TILE = 512
RELAY_WINDOW = RELAY_TILE + LANES
RELAY_ROWS = 1024
_N_BIG = 17
_N_KERNEL_TILES = 2 * _N_BIG - 1
_TAIL_START = _N_KERNEL_TILES * RELAY_TILE


def _relayout_kernel(src_ref, roll_ref, w_ref, tail_ref, o_ref):
  t = pl.program_id(1)

  @pl.when(t < _N_KERNEL_TILES)
  def _():
    o_ref[...] = pltpu.roll(w_ref[...], roll_ref[t], 1)[:, :RELAY_TILE].astype(o_ref.dtype)

  @pl.when(t >= _N_KERNEL_TILES)
  def _():
    o_ref[...] = tail_ref[...]


def relayout_in_proj_weights(w_in):
  depth, d, _ = w_in.shape
  starts = [src + half * RELAY_TILE for _, src, _ in _SRC[:_N_BIG] for half in range(2)]
  starts = starts[:_N_KERNEL_TILES]
  src_blocks = jnp.asarray([s // LANES for s in starts], jnp.int32)
  rolls = jnp.asarray([(RELAY_WINDOW - s % LANES) % RELAY_WINDOW for s in starts], jnp.int32)
  tail = relayout_columns(w_in)[..., _TAIL_START:].astype(BF16)
  n_tail = (P_PAD - _TAIL_START) // RELAY_TILE
  return pl.pallas_call(
      _relayout_kernel,
      grid_spec=pltpu.PrefetchScalarGridSpec(
          num_scalar_prefetch=2,
          grid=(depth, _N_KERNEL_TILES + n_tail, d // RELAY_ROWS),
          in_specs=[
              pl.BlockSpec((None, pl.Element(RELAY_ROWS), pl.Element(RELAY_WINDOW)),
                           lambda l, t, i, src, rl: (
                               l, i * RELAY_ROWS,
                               src[jnp.minimum(t, _N_KERNEL_TILES - 1)] * LANES)),
              pl.BlockSpec((None, RELAY_ROWS, RELAY_TILE),
                           lambda l, t, i, src, rl: (l, i, jnp.maximum(t - _N_KERNEL_TILES, 0))),
          ],
          out_specs=pl.BlockSpec((None, RELAY_ROWS, RELAY_TILE),
                                 lambda l, t, i, src, rl: (l, i, t))),
      out_shape=jax.ShapeDtypeStruct((depth, d, P_PAD), BF16),
      compiler_params=pltpu.CompilerParams(
          dimension_semantics=("parallel", "parallel", "parallel"),
          vmem_limit_bytes=VMEM_LIMIT),
      name="relayout_in_proj_weights",
  )(src_blocks, rolls, w_in, tail)


def _iota(shape, dim):
  return lax.broadcasted_iota(jnp.int32, shape, dim)


def _dot(a, b):
  return jnp.dot(a.astype(BF16), b.astype(BF16), preferred_element_type=F32)


def _dot_nt(a, b):
  return lax.dot_general(a.astype(BF16), b.astype(BF16), (((1,), (1,)), ((), ())),
                         preferred_element_type=F32)


def _dot_tn(a, b):
  return lax.dot_general(a.astype(BF16), b.astype(BF16), (((0,), (0,)), ((), ())),
                         preferred_element_type=F32)


def _bf16_terms(x, n_terms=3):
  terms, rest = [], x
  for _ in range(n_terms):
    t = rest.astype(BF16)
    terms.append(t)
    rest = rest - t.astype(F32)
  return terms


def _mask_dot(mask, x, n_terms=3):
  m = mask.astype(BF16)
  return functools.reduce(
      jnp.add, [jnp.dot(m, t, preferred_element_type=F32) for t in _bf16_terms(x, n_terms)])


def _sigmoid(x):
  return 1.0 / (1.0 + jnp.exp(-x))


def _silu(x):
  return x * _sigmoid(x)


def _softplus(x):
  return jnp.maximum(x, 0.0) + jnp.log1p(jnp.exp(-jnp.abs(x)))


def _shift_rows(x, prev8, j):
  rolled = pltpu.roll(x, j, 0)
  top = jnp.where(_iota((SUBLANES, x.shape[1]), 0) < j, pltpu.roll(prev8, j, 0),
                  rolled[:SUBLANES])
  return jnp.concatenate([top, rolled[SUBLANES:]], axis=0)


def _causal_conv_silu(x_ref, w_ref, prev_ref, slot):
  x = x_ref[...]
  prev8 = prev_ref[slot]
  acc = x * w_ref[CONV_K - 1:CONV_K, :]
  for j in range(1, CONV_K):
    acc = acc + _shift_rows(x, prev8, j) * w_ref[CONV_K - 1 - j:CONV_K - j, :]
  prev_ref[slot] = x[x.shape[0] - SUBLANES:]
  return _silu(acc)


def _tri(n):
  r, c = _iota((n, n), 0), _iota((n, n), 1)
  return r, c


def _chunk_masks(rows):
  r, c = _tri(rows)
  same = (r >> _LOG2_CHUNK) == (c >> _LOG2_CHUNK)
  return same & (c < r), same & (c <= r)


def _inv_unit_lower(mats):
  n = mats[0].shape[0]
  r, c = _tri(n)
  same = lambda log2_size: (r >> log2_size) == (c >> log2_size)
  base = 3
  mm = lambda x, y: jnp.dot(x.astype(BF16), y.astype(BF16), preferred_element_type=F32)
  diag = [jnp.where(same(base), a, 0.0).astype(BF16) for a in mats]
  xs = [jnp.where(r == c, 1.0, 0.0) - d for d in diag]
  ps = [mm(d, d) for d in diag]
  xs = [x + mm(x, p) for x, p in zip(xs, ps)]
  ps = [mm(p, p) for p in ps]
  xs = [x + mm(x, p) for x, p in zip(xs, ps)]
  for level in range(base, _LOG2_CHUNK):
    sibling = same(level + 1) & jnp.logical_not(same(level))
    offs = [jnp.where(sibling, a, 0.0) for a in mats]
    half = [mm(x, o) for x, o in zip(xs, offs)]
    xs = [x - mm(h, x) for x, h in zip(xs, half)]
  return xs


def _select_lane(x, lane_idx):
  return jnp.sum(jnp.where(_iota(x.shape, 1) == lane_idx, x, 0.0), axis=-1, keepdims=True)


def _select_row(x, row_idx):
  return jnp.sum(jnp.where(_iota(x.shape, 0) == row_idx, x, 0.0), axis=0, keepdims=True)


def _rows_of_chunk(x, ci, total_rows):
  parts = []
  if ci:
    parts.append(jnp.zeros((ci * CHUNK, x.shape[1]), x.dtype))
  parts.append(x)
  rest = total_rows - (ci + 1) * CHUNK
  if rest:
    parts.append(jnp.zeros((rest, x.shape[1]), x.dtype))
  return jnp.concatenate(parts, axis=0)


def _head_rms(o, g):
  return o * lax.rsqrt(jnp.mean(o * o, axis=-1, keepdims=True) + NORM_EPS) * g


def _rwkv_kernel(r_ref, k_ref, v_ref, lo_ref, z_ref, mur_ref, muk_ref, muv_ref, mulo_ref,
                 w0_ref, a0_ref, kkw_ref, kaw_ref, rkw_ref, lng_ref, lnb_ref, wup_ref, aup_ref,
                 o_ref, s_ref, prev_ref, prevlo_ref):
  bt, width = r_ref.shape
  n = CHUNK
  n_pairs = width // LANES
  n_chunks = bt // n

  @pl.when(pl.program_id(1) == 0)
  def _():
    s_ref[...] = jnp.zeros_like(s_ref)
    prev_ref[...] = jnp.zeros_like(prev_ref)
    prevlo_ref[...] = jnp.zeros_like(prevlo_ref)

  def token_shift(x_ref, mu_ref, carry_ref, slot):
    x = x_ref[...]
    prev = _shift_rows(x, carry_ref[slot], 1)
    carry_ref[slot] = x[bt - SUBLANES:]
    return x + (prev - x) * mu_ref[...]

  r = token_shift(r_ref, mur_ref, prev_ref, 0)
  k = token_shift(k_ref, muk_ref, prev_ref, 1)
  v = token_shift(v_ref, muv_ref, prev_ref, 2)
  lo = token_shift(lo_ref, mulo_ref, prevlo_ref, 0)

  lo_act = jnp.where(_iota((bt, LANES), 1) < RW_LORA, jnp.tanh(lo), lo)
  log_w = -RW_DECAY_SCALE * _sigmoid(w0_ref[...] + _dot(lo_act, wup_ref[...]))
  a = _sigmoid(a0_ref[...] + _dot(lo_act, aup_ref[...]))

  hr, hc = _tri(width)
  head_ones = jnp.where(hr >> _LOG2_RW_HEAD == hc >> _LOG2_RW_HEAD, 1.0, 0.0)
  kkp = k * kkw_ref[...]
  k2 = k * (1.0 + (a - 1.0) * kaw_ref[...])
  sums = _dot(jnp.concatenate([kkp * kkp, r * k2 * rkw_ref[...]], axis=0), head_ones)
  kk = kkp * lax.rsqrt(sums[:bt] + L2_EPS)
  b = kk * a
  bonus = sums[bt:] * v

  strict, incl = _chunk_masks(bt)
  lc = _mask_dot(jnp.where(incl, 1.0, 0.0), log_w, 2)
  g_in = jnp.exp(lc)
  g_inv = jnp.exp(-lc)
  kap_all = kk * jnp.exp(lc - log_w)
  rt_all = r * g_in
  kh_all = k2 * g_inv
  bh_all = b * g_inv

  lane = _iota((bt, LANES), 1)
  lane_n = _iota((n, LANES), 1)
  head_masks = [jnp.where(lane < RW_HEAD, 1.0, 0.0), jnp.where(lane >= RW_HEAD, 1.0, 0.0)]
  chunk_head_masks = [jnp.where(lane_n < RW_HEAD, 1.0, 0.0), jnp.where(lane_n >= RW_HEAD, 1.0, 0.0)]
  pr, pc = _tri(LANES)
  pair_diag = (pr >> _LOG2_RW_HEAD) == (pc >> _LOG2_RW_HEAD)

  def pair(x, p):
    return x[:, p * LANES:(p + 1) * LANES]

  vheads = [(p, h) for p in range(n_pairs) for h in range(2)]
  big = [_dot_nt(jnp.concatenate([pair(kap_all, p), pair(rt_all, p)], axis=0),
                 jnp.concatenate([pair(kh_all, p) * head_masks[h],
                                  pair(bh_all, p) * head_masks[h]], axis=0))
         for p, h in vheads]
  a_kk = [jnp.where(strict, m[:bt, :bt], 0.0) for m in big]
  a_bk = [jnp.where(strict, m[:bt, bt:], 0.0) for m in big]
  a_kr = [jnp.where(incl, m[bt:, :bt], 0.0) for m in big]
  a_br = [jnp.where(incl, m[bt:, bt:], 0.0) for m in big]
  t_inv = _inv_unit_lower(a_bk)
  av = [_dot(jnp.concatenate([a_kk[i], a_kr[i]], axis=0), pair(v, p) * head_masks[h])
        for i, (p, h) in enumerate(vheads)]
  xy = [_dot(t_inv[i], jnp.concatenate([pair(kap_all, p) * head_masks[h], av[i][:bt]], axis=1))
        for i, (p, h) in enumerate(vheads)]
  x_p = [xy[2 * p][:, :LANES] + xy[2 * p + 1][:, :LANES] for p in range(n_pairs)]
  y_p = [xy[2 * p][:, LANES:] + xy[2 * p + 1][:, LANES:] for p in range(n_pairs)]
  akrv_p = [av[2 * p][bt:] + av[2 * p + 1][bt:] for p in range(n_pairs)]

  s = [s_ref[p] for p in range(n_pairs)]
  outs = [[] for _ in range(n_pairs)]
  for ci in range(n_chunks):
    sl = slice(ci * n, (ci + 1) * n)
    g_last = [pair(g_in, p)[(ci + 1) * n - 1:(ci + 1) * n] for p in range(n_pairs)]
    reads = [_dot_nt(jnp.concatenate([x_p[p][sl], pair(rt_all, p)[sl]], axis=0), s[p])
             for p in range(n_pairs)]
    u = [reads[p][:n] + y_p[p][sl] for p in range(n_pairs)]
    for p in range(n_pairs):
      u_rows = jnp.concatenate(
          [_rows_of_chunk(u[p] * chunk_head_masks[h], ci, bt) for h in range(2)], axis=0)
      a_br_rows = jnp.concatenate([a_br[2 * p][sl], a_br[2 * p + 1][sl]], axis=1)
      outs[p].append(reads[p][n:] + akrv_p[p][sl] - _dot(a_br_rows, u_rows))
      delta = _dot_tn(jnp.concatenate([pair(v, p)[sl], u[p]], axis=0),
                      jnp.concatenate([pair(kh_all, p)[sl] * g_last[p],
                                       -pair(bh_all, p)[sl] * g_last[p]], axis=0))
      s[p] = s[p] * g_last[p] + jnp.where(pair_diag, delta, 0.0)
  for p in range(n_pairs):
    s_ref[p] = s[p]
  o = jnp.concatenate([jnp.concatenate(outs[p], axis=0) for p in range(n_pairs)], axis=1)

  inv_n = 1.0 / RW_HEAD
  mean = _dot(o, head_ones) * inv_n
  d = o - mean
  var = _dot(d * d, head_ones) * inv_n
  o = d * lax.rsqrt(var + RW_GN_EPS) * lng_ref[...] + lnb_ref[...] + bonus
  o_ref[...] = (o * _silu(z_ref[...])).astype(o_ref.dtype)


RW_PAIRS_PER_STEP = 4


def rwkv7_mixer(y, mu, w0, w_up, a0, a_up, k_k, k_a, r_k, ln_g, ln_b, *, bt):
  t = y.shape[0]
  width = RW_PAIRS_PER_STEP * LANES
  groups = W_MIX // width
  col = lambda name: pl.BlockSpec((bt, width), lambda g, i, c=_COL[name] // width: (i, c + g))
  vec = lambda: pl.BlockSpec((1, width), lambda g, i: (0, g))
  zeros = jnp.zeros((RW_LORA, W_MIX), F32)
  wup_p = jnp.concatenate([w_up, zeros], axis=0)
  aup_p = jnp.concatenate([zeros, a_up], axis=0)
  row = lambda p: p.reshape(1, -1)
  return pl.pallas_call(
      _rwkv_kernel,
      grid=(groups, t // bt),
      in_specs=[
          col("A_r"), col("A_k"), col("A_v"),
          pl.BlockSpec((bt, LANES), lambda g, i, c=_COL["A_lo"] // LANES: (i, c)),
          col("A_z"),
          vec(), vec(), vec(), pl.BlockSpec((1, LANES), lambda g, i: (0, 0)),
          vec(), vec(), vec(), vec(), vec(), vec(), vec(),
          pl.BlockSpec((LANES, width), lambda g, i: (0, g)),
          pl.BlockSpec((LANES, width), lambda g, i: (0, g)),
      ],
      out_specs=pl.BlockSpec((bt, width), lambda g, i: (i, g)),
      out_shape=jax.ShapeDtypeStruct((t, W_MIX), BF16),
      scratch_shapes=[pltpu.VMEM((RW_PAIRS_PER_STEP, LANES, LANES), F32),
                      pltpu.VMEM((3, SUBLANES, width), F32),
                      pltpu.VMEM((1, SUBLANES, LANES), F32)],
      compiler_params=pltpu.CompilerParams(
          dimension_semantics=("parallel", "arbitrary"), vmem_limit_bytes=VMEM_LIMIT),
      name="rwkv7_mixer",
  )(y, y, y, y, y, row(mu[:W_MIX]), row(mu[W_MIX:2 * W_MIX]), row(mu[2 * W_MIX:3 * W_MIX]),
    row(mu[3 * W_MIX:]), row(w0), row(a0), row(k_k), row(k_a), row(r_k), row(ln_g), row(ln_b),
    wup_p, aup_p)


def _gdn_kernel(q_ref, k_ref, v_ref, g_ref, z_ref, cq_ref, ck_ref, cv_ref, alog_ref, dtb_ref,
                ng_ref, o_ref, s_ref, prev_ref):
  bt, width = q_ref.shape
  n = CHUNK
  n_heads = width // LANES
  n_chunks = bt // n
  first_head = pl.program_id(0) * n_heads

  @pl.when(pl.program_id(1) == 0)
  def _():
    s_ref[...] = jnp.zeros_like(s_ref)
    prev_ref[...] = jnp.zeros_like(prev_ref)

  def l2n(x):
    return x * lax.rsqrt(jnp.sum(x * x, axis=-1, keepdims=True) + L2_EPS)

  def head(x, j):
    return x[:, j * LANES:(j + 1) * LANES]

  q_all = _causal_conv_silu(q_ref, cq_ref, prev_ref, 0)
  k_all = _causal_conv_silu(k_ref, ck_ref, prev_ref, 1)
  v_all = _causal_conv_silu(v_ref, cv_ref, prev_ref, 2)
  gates = g_ref[...]
  beta_all = _sigmoid(gates)
  log_a_all = -jnp.exp(alog_ref[...]) * _softplus(gates + dtb_ref[...])
  strict, incl = _chunk_masks(bt)
  g_all = _mask_dot(jnp.where(incl, 1.0, 0.0), log_a_all, 2)
  g_all_t = g_all.T

  qs, ks, vs, kbs, g_cols, decays = [], [], [], [], [], []
  for j in range(n_heads):
    beta = _select_lane(beta_all, first_head + j)
    g_col = _select_lane(g_all, GD_H + first_head + j)
    g_row = _select_row(g_all_t, GD_H + first_head + j)
    decays.append(jnp.exp(jnp.minimum(g_col - g_row, 0.0)))
    k = l2n(head(k_all, j))
    qs.append(l2n(head(q_all, j)) * (GD_HD ** -0.5))
    ks.append(k)
    kbs.append(k * beta)
    vs.append(head(v_all, j) * beta)
    g_cols.append(g_col)
  heads = range(n_heads)
  big = [_dot_nt(jnp.concatenate([kbs[j], qs[j]], axis=0), ks[j]) for j in heads]
  m = [jnp.where(strict, big[j][:bt] * decays[j], 0.0) for j in heads]
  attn = [jnp.where(incl, big[j][bt:] * decays[j], 0.0) for j in heads]
  t_inv = _inv_unit_lower(m)
  e_g = [jnp.exp(g) for g in g_cols]
  yx = [_dot(t_inv[j], jnp.concatenate([vs[j], kbs[j] * e_g[j]], axis=1)) for j in heads]
  qe = [qs[j] * e_g[j] for j in heads]

  s = [s_ref[j] for j in heads]
  outs = [[] for _ in heads]
  for ci in range(n_chunks):
    sl = slice(ci * n, (ci + 1) * n)
    reads = [_dot(jnp.concatenate([yx[j][sl, LANES:], qe[j][sl]], axis=0), s[j]) for j in heads]
    for j in heads:
      v_new = yx[j][sl, :LANES] - reads[j][:n]
      outs[j].append(reads[j][n:] + _dot(attn[j][sl], _rows_of_chunk(v_new, ci, bt)))
      g_end = g_cols[j][(ci + 1) * n - 1:(ci + 1) * n]
      s[j] = jnp.exp(g_end) * s[j] + _dot_tn(ks[j][sl] * jnp.exp(g_end - g_cols[j][sl]), v_new)
  for j in heads:
    s_ref[j] = s[j]
  o = jnp.concatenate(
      [_head_rms(jnp.concatenate(outs[j], axis=0), head(ng_ref[...], j)) for j in heads], axis=1)
  o_ref[...] = (o * _silu(z_ref[...])).astype(o_ref.dtype)


GD_HEADS_PER_STEP = 4


def gdn_mixer(y, conv_w, a_log, dt_bias, norm_g, *, bt):
  t = y.shape[0]
  width = GD_HEADS_PER_STEP * LANES
  groups = W_MIX // width
  col = lambda name: pl.BlockSpec((bt, width), lambda g, i, c=_COL[name] // width: (i, c + g))
  cw = lambda part: pl.BlockSpec((CONV_K, width), lambda g, i, p=part: (0, p * groups + g))
  fix = pl.BlockSpec((1, LANES), lambda g, i: (0, 0))
  on_gate_lanes = lambda p: jnp.zeros((1, LANES), F32).at[0, GD_H:2 * GD_H].set(p)
  return pl.pallas_call(
      _gdn_kernel,
      grid=(groups, t // bt),
      in_specs=[
          col("D_q"), col("D_k"), col("D_v"),
          pl.BlockSpec((bt, LANES), lambda g, i, c=_COL["D_g"] // LANES: (i, c)),
          col("D_z"), cw(0), cw(1), cw(2), fix, fix,
          pl.BlockSpec((1, width), lambda g, i: (0, g)),
      ],
      out_specs=pl.BlockSpec((bt, width), lambda g, i: (i, g)),
      out_shape=jax.ShapeDtypeStruct((t, W_MIX), BF16),
      scratch_shapes=[pltpu.VMEM((GD_HEADS_PER_STEP, GD_HD, GD_HD), F32),
                      pltpu.VMEM((3, SUBLANES, width), F32)],
      compiler_params=pltpu.CompilerParams(
          dimension_semantics=("parallel", "arbitrary"), vmem_limit_bytes=VMEM_LIMIT),
      name="gdn_mixer",
  )(y, y, y, y, y, conv_w, conv_w, conv_w, on_gate_lanes(a_log), on_gate_lanes(dt_bias),
    norm_g.reshape(1, -1))


def _mlstm_kernel(q_ref, k_ref, v_ref, og_ref, z_ref, g_ref, cq_ref, ck_ref, ib_ref, fb_ref,
                  ng_ref, o_ref, c_ref, n_ref, m_ref, prev_ref):
  bt = q_ref.shape[0]
  n = CHUNK
  h = pl.program_id(0)

  @pl.when(pl.program_id(1) == 0)
  def _():
    c_ref[...] = jnp.zeros_like(c_ref)
    n_ref[...] = jnp.zeros_like(n_ref)
    m_ref[...] = jnp.zeros_like(m_ref)
    prev_ref[...] = jnp.zeros_like(prev_ref)

  q = _causal_conv_silu(q_ref, cq_ref, prev_ref, 0)
  k = _causal_conv_silu(k_ref, ck_ref, prev_ref, 1) * (ML_HD ** -0.5)
  v = v_ref[...]
  gates = g_ref[...]
  log_i = _select_lane(gates, h) + ib_ref[...]
  log_f = -_softplus(-(_select_lane(gates, ML_H + h) + fb_ref[...]))

  tr, tc = _tri(n)
  causal = tc <= tr
  lower_incl = jnp.where(causal, 1.0, 0.0)
  upper_incl = jnp.where(tr <= tc, 1.0, 0.0)
  eye = jnp.where(tr == tc, 1.0, 0.0)
  ones = jnp.ones((n, n), F32)

  c_st, n_st, m_st = c_ref[...], n_ref[...], m_ref[...]
  outs = []
  for ci in range(bt // n):
    sl = slice(ci * n, (ci + 1) * n)
    fc, ic = log_f[sl], log_i[sl]
    g_col = _mask_dot(lower_incl, fc)
    g_row = _mask_dot(ones, upper_incl * fc[:, :n])
    i_row = _mask_dot(ones, eye * ic[:, :n])
    dmat = jnp.where(causal, g_col[:, :n] - g_row + i_row, -jnp.inf)
    inter = g_col + m_st
    m_row = jnp.maximum(jnp.max(dmat, axis=-1, keepdims=True), inter)
    qc, kc, vc = q[sl], k[sl], v[sl]
    sc = _dot_nt(qc, kc) * jnp.exp(dmat - m_row[:, :n])
    w_inter = jnp.exp(inter - m_row)[:, :1]
    num = _dot(sc, vc) + w_inter * _dot(qc, c_st)
    den = (jnp.sum(sc, axis=-1, keepdims=True)
           + w_inter * jnp.sum(qc * n_st, axis=-1, keepdims=True))
    outs.append(num / jnp.maximum(jnp.abs(den), jnp.exp(-m_row[:, :1])))
    g_end = g_col[n - 1:n]
    log_w = g_end - g_col + ic
    m_new = jnp.maximum(g_end + m_st, jnp.max(log_w, axis=0, keepdims=True))
    carry = jnp.exp(g_end + m_st - m_new)[:, :1]
    wk = jnp.exp(log_w - m_new)[:, :1] * kc
    c_st = carry * c_st + _dot_tn(wk, vc)
    n_st = carry * n_st + jnp.sum(wk, axis=0, keepdims=True)
    m_st = m_new
  c_ref[...] = c_st
  n_ref[...] = n_st
  m_ref[...] = m_st
  hh = _head_rms(jnp.concatenate(outs, axis=0), ng_ref[...]) * _sigmoid(og_ref[...])
  o_ref[...] = (hh * _silu(z_ref[...])).astype(o_ref.dtype)


def mlstm_mixer(y, conv_w, i_bias, f_bias, norm_g, *, bt):
  t = y.shape[0]
  cb = lambda name: _COL[name] // ML_HD
  col = lambda name: pl.BlockSpec((bt, ML_HD), lambda h, i, c=cb(name): (i, c + h))
  cw = lambda part: pl.BlockSpec((CONV_K, ML_HD), lambda h, i, p=part: (0, p * ML_H + h))
  scal = lambda: pl.BlockSpec((None, 1, LANES), lambda h, i: (h, 0, 0))
  bcast = lambda p: jnp.broadcast_to(p[:, None, None], (ML_H, 1, LANES))
  return pl.pallas_call(
      _mlstm_kernel,
      grid=(ML_H, t // bt),
      in_specs=[
          col("B_q"), col("B_k"), col("B_v"), col("B_o"), col("B_z"),
          pl.BlockSpec((bt, LANES), lambda h, i, c=_COL["B_g"] // LANES: (i, c)),
          cw(0), cw(1), scal(), scal(),
          pl.BlockSpec((1, ML_HD), lambda h, i: (0, h)),
      ],
      out_specs=pl.BlockSpec((bt, ML_HD), lambda h, i: (i, h)),
      out_shape=jax.ShapeDtypeStruct((t, W_MIX), BF16),
      scratch_shapes=[pltpu.VMEM((ML_HD, ML_HD), F32), pltpu.VMEM((1, ML_HD), F32),
                      pltpu.VMEM((1, LANES), F32), pltpu.VMEM((2, SUBLANES, ML_HD), F32)],
      compiler_params=pltpu.CompilerParams(
          dimension_semantics=("parallel", "arbitrary"), vmem_limit_bytes=VMEM_LIMIT),
      name="mlstm_mixer",
  )(y, y, y, y, y, y, conv_w, conv_w, bcast(i_bias), bcast(f_bias), norm_g.reshape(1, -1))


HG_SUB = 16


def _hgrn2_kernel(q_ref, f_ref, i_ref, z_ref, lb_ref, ng_ref, o_ref, st_ref):
  bt = q_ref.shape[0]
  n = CHUNK
  n_chunks = bt // n

  @pl.when(pl.program_id(1) == 0)
  def _():
    st_ref[...] = jnp.zeros_like(st_ref)

  lb = lb_ref[...]
  f = lb + (1.0 - lb) * _sigmoid(f_ref[...])
  q = _silu(q_ref[...])
  k = 1.0 - f
  log_f = jnp.log(f)
  v = i_ref[...]

  def group_row(x, size, r):
    x3 = x.reshape(bt // size, size, HG_HD)
    return jnp.broadcast_to(x3[:, r:r + 1, :], x3.shape).reshape(bt, HG_HD)

  strict, incl = _chunk_masks(bt)
  b = _mask_dot(jnp.where(incl, 1.0, 0.0), log_f, 2)
  b_ex = b - log_f
  row = _iota((bt, HG_HD), 0)
  row_in_block = row & (HG_SUB - 1)
  row_in_chunk = row & (n - 1)

  o = jnp.zeros((bt, HG_HD), F32)
  for l in range(HG_SUB):
    e = jnp.exp(jnp.where(row_in_block >= l, b - group_row(b, HG_SUB, l), -jnp.inf))
    o = o + (jnp.sum(q * group_row(k, HG_SUB, l) * e, axis=-1, keepdims=True)
             * group_row(v, HG_SUB, l))

  qs, ks = [], []
  for bi in range(1, n // HG_SUB):
    start = bi * HG_SUB
    ref = group_row(b_ex, n, start)
    in_block = (row_in_chunk >= start) & (row_in_chunk < start + HG_SUB)
    qs.append(q * jnp.exp(jnp.where(in_block, b - ref, -jnp.inf)))
    ks.append(k * jnp.exp(jnp.where(row_in_chunk < start, ref - b, -jnp.inf)))
  a_off = _dot_nt(jnp.concatenate(qs, axis=1), jnp.concatenate(ks, axis=1))
  o = o + _dot(jnp.where(strict, a_off, 0.0), v)

  qe = q * jnp.exp(b)
  kd = k * jnp.exp(group_row(b, n, n - 1) - b)
  st = st_ref[...]
  reads = []
  for ci in range(n_chunks):
    sl = slice(ci * n, (ci + 1) * n)
    reads.append(_dot_nt(qe[sl], st))
    st = st * jnp.exp(b[(ci + 1) * n - 1:(ci + 1) * n]) + _dot_tn(v[sl], kd[sl])
  st_ref[...] = st
  o = o + jnp.concatenate(reads, axis=0)
  o_ref[...] = (_head_rms(o, ng_ref[...]) * _silu(z_ref[...])).astype(o_ref.dtype)


def hgrn2_mixer(y, lower_bound, norm_g, *, bt):
  t = y.shape[0]
  cb = lambda name: _COL[name] // LANES
  col = lambda name: pl.BlockSpec((bt, LANES), lambda h, i, c=cb(name): (i, c + h))
  vec = lambda: pl.BlockSpec((1, LANES), lambda h, i: (0, h))
  return pl.pallas_call(
      _hgrn2_kernel,
      grid=(HG_H, t // bt),
      in_specs=[col("C_q"), col("C_f"), col("C_i"), col("C_z"), vec(), vec()],
      out_specs=pl.BlockSpec((bt, LANES), lambda h, i: (i, h)),
      out_shape=jax.ShapeDtypeStruct((t, W_MIX), BF16),
      scratch_shapes=[pltpu.VMEM((HG_HD, HG_HD), F32)],
      compiler_params=pltpu.CompilerParams(
          dimension_semantics=("parallel", "arbitrary"), vmem_limit_bytes=VMEM_LIMIT),
      name="hgrn2_mixer",
  )(y, y, y, y, lower_bound.reshape(1, -1), norm_g.reshape(1, -1))


def _lower_bounds_kernel(x_ref, o_ref):
  x = x_ref[...]
  e = jnp.exp(x - jnp.max(x, axis=0, keepdims=True))
  p = e / jnp.sum(e, axis=0, keepdims=True)
  acc = jnp.zeros_like(p[0:1])
  for l in range(DEPTH):
    o_ref[l:l + 1, :] = acc
    if l + 1 < DEPTH:
      acc = acc + p[l + 1:l + 2]


def hgrn2_lower_bounds(hg_lower):
  return pl.pallas_call(
      _lower_bounds_kernel,
      out_shape=jax.ShapeDtypeStruct(hg_lower.shape, F32),
      name="hgrn2_lower_bounds",
  )(hg_lower)


def _ada_kernel(c_ref, w_ref, b_ref, o_ref):
  @pl.when(pl.program_id(1) == 0)
  def _():
    o_ref[...] = b_ref[...]

  ca = _silu(c_ref[...])
  n_out = w_ref.shape[1]
  cols = [jnp.sum(w_ref[:, j * LANES:(j + 1) * LANES] * ca, axis=0, keepdims=True)
          for j in range(n_out // LANES)]
  o_ref[...] += jnp.concatenate(cols, axis=1)


def adaln_params(c, w_ada, b_ada, *, tk=128):
  depth, d, n_out = w_ada.shape
  c_cols = jnp.broadcast_to(c.reshape(d, 1), (d, LANES))
  return pl.pallas_call(
      _ada_kernel,
      grid=(depth, d // tk),
      in_specs=[
          pl.BlockSpec((tk, LANES), lambda l, k: (k, 0)),
          pl.BlockSpec((None, tk, n_out), lambda l, k: (l, k, 0)),
          pl.BlockSpec((None, 1, n_out), lambda l, k: (l, 0, 0)),
      ],
      out_specs=pl.BlockSpec((None, 1, n_out), lambda l, k: (l, 0, 0)),
      out_shape=jax.ShapeDtypeStruct((depth, 1, n_out), F32),
      compiler_params=pltpu.CompilerParams(
          dimension_semantics=("parallel", "arbitrary"), vmem_limit_bytes=VMEM_LIMIT),
      name="adaln_params",
  )(c_cols, w_ada, b_ada.reshape(depth, 1, n_out))


def _norm_mod_kernel(x_ref, g_ref, shift_ref, scale_ref, o_ref):
  x = x_ref[...]
  y = x * lax.rsqrt(jnp.mean(x * x, axis=-1, keepdims=True) + NORM_EPS) * g_ref[...]
  o_ref[...] = (y * (1.0 + scale_ref[...]) + shift_ref[...]).astype(o_ref.dtype)


def norm_modulate(x, g, shift, scale, *, tm=256):
  t, d = x.shape
  vec = pl.BlockSpec((1, d), lambda i: (0, 0))
  return pl.pallas_call(
      _norm_mod_kernel,
      grid=(t // tm,),
      in_specs=[pl.BlockSpec((tm, d), lambda i: (i, 0)), vec, vec, vec],
      out_specs=pl.BlockSpec((tm, d), lambda i: (i, 0)),
      out_shape=jax.ShapeDtypeStruct((t, d), BF16),
      compiler_params=pltpu.CompilerParams(
          dimension_semantics=("parallel",), vmem_limit_bytes=VMEM_LIMIT),
      name="norm_modulate",
  )(x, g.reshape(1, d), shift, scale)


def _final_norm_kernel(x_ref, g_ref, o_ref):
  x = x_ref[...]
  o_ref[...] = x * lax.rsqrt(jnp.mean(x * x, axis=-1, keepdims=True) + NORM_EPS) * g_ref[...]


def final_norm(x, g, *, tm=256):
  t, d = x.shape
  return pl.pallas_call(
      _final_norm_kernel,
      grid=(t // tm,),
      in_specs=[pl.BlockSpec((tm, d), lambda i: (i, 0)), pl.BlockSpec((1, d), lambda i: (0, 0))],
      out_specs=pl.BlockSpec((tm, d), lambda i: (i, 0)),
      out_shape=jax.ShapeDtypeStruct((t, d), F32),
      compiler_params=pltpu.CompilerParams(
          dimension_semantics=("parallel",), vmem_limit_bytes=VMEM_LIMIT),
      name="final_norm",
  )(x, g.reshape(1, d))


def _in_proj_kernel(h_ref, w_ref, o_ref):
  o_ref[...] = jnp.dot(h_ref[...], w_ref[...], preferred_element_type=F32)


def in_proj(h, w_all, layer, *, tm=512, tn=1280):
  t, d = h.shape
  n_out = w_all.shape[2]
  return pl.pallas_call(
      _in_proj_kernel,
      grid=(n_out // tn, t // tm),
      in_specs=[pl.BlockSpec((tm, d), lambda j, i: (i, 0)),
                pl.BlockSpec((None, d, tn), lambda j, i: (layer, 0, j))],
      out_specs=pl.BlockSpec((tm, tn), lambda j, i: (i, j)),
      out_shape=jax.ShapeDtypeStruct((t, n_out), F32),
      compiler_params=pltpu.CompilerParams(
          dimension_semantics=("parallel", "parallel"), vmem_limit_bytes=VMEM_LIMIT),
      name="in_proj",
  )(h, w_all)


def _out_proj_kernel(oa_ref, ob_ref, oc_ref, od_ref, w_ref, x_ref, gate_ref, o_ref, wb_ref):
  @pl.when(pl.program_id(1) == 0)
  def _():
    wb_ref[...] = w_ref[...].astype(BF16)

  acc = jnp.dot(oa_ref[...], wb_ref[0:W_MIX, :], preferred_element_type=F32)
  for g, ref in enumerate((ob_ref, oc_ref, od_ref), start=1):
    acc = acc + jnp.dot(ref[...], wb_ref[g * W_MIX:(g + 1) * W_MIX, :],
                        preferred_element_type=F32)
  o_ref[...] = x_ref[...] + gate_ref[...] * acc


def out_proj_residual(o_groups, w_all, layer, x, gate, *, tm=512, tn=1024):
  t, d = x.shape
  og = pl.BlockSpec((tm, W_MIX), lambda j, i: (i, 0))
  return pl.pallas_call(
      _out_proj_kernel,
      grid=(d // tn, t // tm),
      in_specs=[og, og, og, og,
                pl.BlockSpec((None, 4 * W_MIX, tn), lambda j, i: (layer, 0, j),
                             pipeline_mode=pl.Buffered(1)),
                pl.BlockSpec((tm, tn), lambda j, i: (i, j)),
                pl.BlockSpec((1, tn), lambda j, i: (0, j))],
      out_specs=pl.BlockSpec((tm, tn), lambda j, i: (i, j)),
      out_shape=jax.ShapeDtypeStruct((t, d), F32),
      scratch_shapes=[pltpu.VMEM((4 * W_MIX, tn), BF16)],
      compiler_params=pltpu.CompilerParams(
          dimension_semantics=("parallel", "arbitrary"), vmem_limit_bytes=VMEM_LIMIT),
      name="out_proj_residual",
  )(*o_groups, w_all, x, gate)


MIXER_BT = 256


def kernel(x, c, norm_g, w_ada, b_ada, w_in, w_out, rw_mu, rw_w0, rw_w_up, rw_a0, rw_a_up, rw_k_k,
           rw_k_a, rw_r_k, rw_ln_g, rw_ln_b, ml_conv, ml_i_bias, ml_f_bias, ml_norm_g, hg_lower,
           hg_norm_g, gd_conv, gd_A_log, gd_dt_bias, gd_norm_g, final_g):
  bsz, seq, d = x.shape
  assert bsz == 1 and d == D_MODEL
  xs = x.reshape(seq, d)
  lower_bounds = hgrn2_lower_bounds(hg_lower)
  ada = adaln_params(c, w_ada, b_ada)
  w_in_p = relayout_in_proj_weights(w_in)
  bt = MIXER_BT
  for l in range(DEPTH):
    shift, scale, gate = ada[l, :, :d], ada[l, :, d:2 * d], ada[l, :, 2 * d:]
    h = norm_modulate(xs, norm_g[l], shift, scale)
    y = in_proj(h, w_in_p, l)
    o_a = rwkv7_mixer(y, rw_mu[l], rw_w0[l], rw_w_up[l], rw_a0[l], rw_a_up[l], rw_k_k[l],
                      rw_k_a[l], rw_r_k[l], rw_ln_g[l], rw_ln_b[l], bt=bt)
    o_b = mlstm_mixer(y, ml_conv[l], ml_i_bias[l], ml_f_bias[l], ml_norm_g[l], bt=2 * bt)
    o_c = hgrn2_mixer(y, lower_bounds[l], hg_norm_g[l], bt=bt)
    o_d = gdn_mixer(y, gd_conv[l], gd_A_log[l], gd_dt_bias[l], gd_norm_g[l], bt=bt)
    xs = out_proj_residual((o_a, o_b, o_c, o_d), w_out, l, xs, gate)
  return final_norm(xs, final_g).reshape(bsz, seq, d)
```

```python
import functools

import jax
import jax.numpy as jnp
from jax import lax
from jax.experimental import pallas as pl
from jax.experimental.pallas import tpu as pltpu

F32 = jnp.float32
BF16 = jnp.bfloat16

D_MODEL = 4096
DEPTH = 4
W_MIX = 1024
RW_HEAD = 64
RW_LORA = 64
RW_DECAY_SCALE = 0.606531
RW_GN_EPS = 64e-5
ML_H = 4
ML_HD = 256
HG_H = 8
HG_HD = 128
GD_H = 8
GD_HD = 128
CONV_K = 4
CHUNK = 64
_LOG2_CHUNK = 6
_LOG2_RW_HEAD = 6
assert 1 << _LOG2_CHUNK == CHUNK and 1 << _LOG2_RW_HEAD == RW_HEAD
NORM_EPS = 1e-6
L2_EPS = 1e-6

LANES = 128
SUBLANES = 8
VMEM_LIMIT = 48 * 1024 * 1024

_COL = dict(
    B_q=0, B_k=1024, B_v=2048, B_o=3072, B_z=4096,
    C_q=5120, C_f=6144, C_i=7168, C_z=8192,
    A_r=9216, A_k=10240, A_v=11264, A_z=12288,
    D_q=13312, D_k=14336, D_v=15360, D_z=16384,
    A_lo=17408, B_g=17536, D_g=17664,
)
P_PAD = 17920
_A0, _B0, _C0, _D0 = 0, 4224, 9352, 13448
_SRC = (
    ("B_q", _B0, 1024), ("B_k", _B0 + 1024, 1024), ("B_v", _B0 + 2048, 1024),
    ("B_o", _B0 + 3072, 1024), ("B_z", _B0 + 4104, 1024),
    ("C_q", _C0, 1024), ("C_f", _C0 + 1024, 1024), ("C_i", _C0 + 2048, 1024),
    ("C_z", _C0 + 3072, 1024),
    ("A_r", _A0, 1024), ("A_k", _A0 + 1024, 1024), ("A_v", _A0 + 2048, 1024),
    ("A_z", _A0 + 3200, 1024),
    ("D_q", _D0, 1024), ("D_k", _D0 + 1024, 1024), ("D_v", _D0 + 2048, 1024),
    ("D_z", _D0 + 3088, 1024),
    ("A_lo", _A0 + 3072, 128), ("B_g", _B0 + 4096, 8), ("D_g", _D0 + 3072, 16),
)


def relayout_columns(w):
  parts, pos = [], 0
  for name, src, width in _SRC:
    assert _COL[name] == pos, (name, pos)
    parts.append(w[..., src:src + width])
    pad = (-width) % LANES
    if pad:
      parts.append(jnp.zeros(w.shape[:-1] + (pad,), w.dtype))
    pos += width + pad
  parts.append(jnp.zeros(w.shape[:-1] + (P_PAD - pos,), w.dtype))
  return jnp.concatenate(parts, axis=-1)


RELAY_TILE = 512
RELAY_K = 1024
_N_BIG = 17
_N_BIG_TILES = _N_BIG * (1024 // RELAY_TILE)
assert _COL["A_lo"] == _N_BIG_TILES * RELAY_TILE and P_PAD == (_N_BIG_TILES + 1) * RELAY_TILE


def _relayout_kernel(src_ref, w_ref, tail_ref, o_ref):
  t = pl.program_id(1)

  @pl.when(t < _N_BIG_TILES)
  def _():
    o_ref[...] = w_ref[...].T.astype(o_ref.dtype)

  @pl.when(t >= _N_BIG_TILES)
  def _():
    o_ref[...] = tail_ref[...].T.astype(o_ref.dtype)


def relayout_in_proj_weights(w_in):
  depth, d, _ = w_in.shape
  wt = jnp.swapaxes(w_in, 1, 2)
  starts = [src + half * RELAY_TILE for _, src, _ in _SRC[:_N_BIG]
            for half in range(1024 // RELAY_TILE)]
  assert all(s % SUBLANES == 0 for s in starts)
  src_rows8 = jnp.asarray([s // SUBLANES for s in starts], jnp.int32)
  parts, pos = [], 0
  for _, src, width in _SRC[_N_BIG:]:
    parts.append(wt[:, src:src + width])
    pad = (-width) % LANES
    if pad:
      parts.append(jnp.zeros((depth, pad, d), wt.dtype))
    pos += width + pad
  parts.append(jnp.zeros((depth, RELAY_TILE - pos, d), wt.dtype))
  tail = jnp.concatenate(parts, axis=1)
  return pl.pallas_call(
      _relayout_kernel,
      grid_spec=pltpu.PrefetchScalarGridSpec(
          num_scalar_prefetch=1,
          grid=(depth, _N_BIG_TILES + 1, d // RELAY_K),
          in_specs=[
              pl.BlockSpec((None, pl.Element(RELAY_TILE), pl.Element(RELAY_K)),
                           lambda l, t, k, src: (
                               l, src[jnp.minimum(t, _N_BIG_TILES - 1)] * SUBLANES, k * RELAY_K)),
              pl.BlockSpec((None, RELAY_TILE, RELAY_K), lambda l, t, k, src: (l, 0, k)),
          ],
          out_specs=pl.BlockSpec((None, RELAY_K, RELAY_TILE), lambda l, t, k, src: (l, k, t))),
      out_shape=jax.ShapeDtypeStruct((depth, d, P_PAD), BF16),
      compiler_params=pltpu.CompilerParams(
          dimension_semantics=("parallel", "parallel", "parallel"),
          vmem_limit_bytes=VMEM_LIMIT),
      name="relayout_in_proj_weights",
  )(src_rows8, wt, tail)


def _iota(shape, dim):
  return lax.broadcasted_iota(jnp.int32, shape, dim)


def _dot(a, b):
  return jnp.dot(a.astype(BF16), b.astype(BF16), preferred_element_type=F32)


def _dot_nt(a, b):
  return lax.dot_general(a.astype(BF16), b.astype(BF16), (((1,), (1,)), ((), ())),
                         preferred_element_type=F32)


def _dot_tn(a, b):
  return lax.dot_general(a.astype(BF16), b.astype(BF16), (((0,), (0,)), ((), ())),
                         preferred_element_type=F32)


def _bf16_terms(x, n_terms=3):
  terms, rest = [], x
  for _ in range(n_terms):
    t = rest.astype(BF16)
    terms.append(t)
    rest = rest - t.astype(F32)
  return terms


def _mask_dot(mask, x, n_terms=3):
  m = mask.astype(BF16)
  return functools.reduce(
      jnp.add, [jnp.dot(m, t, preferred_element_type=F32) for t in _bf16_terms(x, n_terms)])


def _sigmoid(x):
  return 1.0 / (1.0 + jnp.exp(-x))


def _silu(x):
  return x * _sigmoid(x)


def _softplus(x):
  return jnp.maximum(x, 0.0) + jnp.log1p(jnp.exp(-jnp.abs(x)))


def _shift_rows(x, prev8, j):
  rolled = pltpu.roll(x, j, 0)
  top = jnp.where(_iota((SUBLANES, x.shape[1]), 0) < j, pltpu.roll(prev8, j, 0),
                  rolled[:SUBLANES])
  return jnp.concatenate([top, rolled[SUBLANES:]], axis=0)


def _causal_conv_silu(x_ref, w_ref, prev_ref, slot):
  x = x_ref[...]
  prev8 = prev_ref[slot]
  acc = x * w_ref[CONV_K - 1:CONV_K, :]
  for j in range(1, CONV_K):
    acc = acc + _shift_rows(x, prev8, j) * w_ref[CONV_K - 1 - j:CONV_K - j, :]
  prev_ref[slot] = x[x.shape[0] - SUBLANES:]
  return _silu(acc)


def _tri(n):
  r, c = _iota((n, n), 0), _iota((n, n), 1)
  return r, c


def _chunk_masks(rows):
  r, c = _tri(rows)
  same = (r >> _LOG2_CHUNK) == (c >> _LOG2_CHUNK)
  return same & (c < r), same & (c <= r)


def _inv_unit_lower(mats):
  n = mats[0].shape[0]
  r, c = _tri(n)
  same = lambda log2_size: (r >> log2_size) == (c >> log2_size)
  base = 3
  mm = lambda x, y: jnp.dot(x.astype(BF16), y.astype(BF16), preferred_element_type=F32)
  diag = [jnp.where(same(base), a, 0.0).astype(BF16) for a in mats]
  xs = [jnp.where(r == c, 1.0, 0.0) - d for d in diag]
  ps = [mm(d, d) for d in diag]
  xs = [x + mm(x, p) for x, p in zip(xs, ps)]
  ps = [mm(p, p) for p in ps]
  xs = [x + mm(x, p) for x, p in zip(xs, ps)]
  for level in range(base, _LOG2_CHUNK):
    sibling = same(level + 1) & jnp.logical_not(same(level))
    offs = [jnp.where(sibling, a, 0.0) for a in mats]
    half = [mm(x, o) for x, o in zip(xs, offs)]
    xs = [x - mm(h, x) for x, h in zip(xs, half)]
  return xs


def _select_lane(x, lane_idx):
  return jnp.sum(jnp.where(_iota(x.shape, 1) == lane_idx, x, 0.0), axis=-1, keepdims=True)


def _select_row(x, row_idx):
  return jnp.sum(jnp.where(_iota(x.shape, 0) == row_idx, x, 0.0), axis=0, keepdims=True)


def _rows_of_chunk(x, ci, total_rows):
  parts = []
  if ci:
    parts.append(jnp.zeros((ci * CHUNK, x.shape[1]), x.dtype))
  parts.append(x)
  rest = total_rows - (ci + 1) * CHUNK
  if rest:
    parts.append(jnp.zeros((rest, x.shape[1]), x.dtype))
  return jnp.concatenate(parts, axis=0)


def _head_rms(o, g):
  return o * lax.rsqrt(jnp.mean(o * o, axis=-1, keepdims=True) + NORM_EPS) * g


def _rwkv_kernel(r_ref, k_ref, v_ref, lo_ref, z_ref, mur_ref, muk_ref, muv_ref, mulo_ref,
                 w0_ref, a0_ref, kkw_ref, kaw_ref, rkw_ref, lng_ref, lnb_ref, wup_ref, aup_ref,
                 o_ref, s_ref, prev_ref, prevlo_ref):
  bt, width = r_ref.shape
  n = CHUNK
  n_pairs = width // LANES
  n_chunks = bt // n

  @pl.when(pl.program_id(1) == 0)
  def _():
    s_ref[...] = jnp.zeros_like(s_ref)
    prev_ref[...] = jnp.zeros_like(prev_ref)
    prevlo_ref[...] = jnp.zeros_like(prevlo_ref)

  def token_shift(x_ref, mu_ref, carry_ref, slot):
    x = x_ref[...]
    prev = _shift_rows(x, carry_ref[slot], 1)
    carry_ref[slot] = x[bt - SUBLANES:]
    return x + (prev - x) * mu_ref[...]

  r = token_shift(r_ref, mur_ref, prev_ref, 0)
  k = token_shift(k_ref, muk_ref, prev_ref, 1)
  v = token_shift(v_ref, muv_ref, prev_ref, 2)
  lo = token_shift(lo_ref, mulo_ref, prevlo_ref, 0)

  lo_act = jnp.where(_iota((bt, LANES), 1) < RW_LORA, jnp.tanh(lo), lo)
  log_w = -RW_DECAY_SCALE * _sigmoid(w0_ref[...] + _dot(lo_act, wup_ref[...]))
  a = _sigmoid(a0_ref[...] + _dot(lo_act, aup_ref[...]))

  hr, hc = _tri(width)
  head_ones = jnp.where(hr >> _LOG2_RW_HEAD == hc >> _LOG2_RW_HEAD, 1.0, 0.0)
  kkp = k * kkw_ref[...]
  k2 = k * (1.0 + (a - 1.0) * kaw_ref[...])
  sums = _dot(jnp.concatenate([kkp * kkp, r * k2 * rkw_ref[...]], axis=0), head_ones)
  kk = kkp * lax.rsqrt(sums[:bt] + L2_EPS)
  b = kk * a
  bonus = sums[bt:] * v

  strict, incl = _chunk_masks(bt)
  lc = _mask_dot(jnp.where(incl, 1.0, 0.0), log_w, 2)
  g_in = jnp.exp(lc)
  g_inv = jnp.exp(-lc)
  kap_all = kk * jnp.exp(lc - log_w)
  rt_all = r * g_in
  kh_all = k2 * g_inv
  bh_all = b * g_inv

  lane = _iota((bt, LANES), 1)
  lane_n = _iota((n, LANES), 1)
  head_masks = [jnp.where(lane < RW_HEAD, 1.0, 0.0), jnp.where(lane >= RW_HEAD, 1.0, 0.0)]
  chunk_head_masks = [jnp.where(lane_n < RW_HEAD, 1.0, 0.0), jnp.where(lane_n >= RW_HEAD, 1.0, 0.0)]
  pr, pc = _tri(LANES)
  pair_diag = (pr >> _LOG2_RW_HEAD) == (pc >> _LOG2_RW_HEAD)

  def pair(x, p):
    return x[:, p * LANES:(p + 1) * LANES]

  vheads = [(p, h) for p in range(n_pairs) for h in range(2)]
  big = [_dot_nt(jnp.concatenate([pair(kap_all, p), pair(rt_all, p)], axis=0),
                 jnp.concatenate([pair(kh_all, p) * head_masks[h],
                                  pair(bh_all, p) * head_masks[h]], axis=0))
         for p, h in vheads]
  a_kk = [jnp.where(strict, m[:bt, :bt], 0.0) for m in big]
  a_bk = [jnp.where(strict, m[:bt, bt:], 0.0) for m in big]
  a_kr = [jnp.where(incl, m[bt:, :bt], 0.0) for m in big]
  a_br = [jnp.where(incl, m[bt:, bt:], 0.0) for m in big]
  t_inv = _inv_unit_lower(a_bk)
  av = [_dot(jnp.concatenate([a_kk[i], a_kr[i]], axis=0), pair(v, p) * head_masks[h])
        for i, (p, h) in enumerate(vheads)]
  xy = [_dot(t_inv[i], jnp.concatenate([pair(kap_all, p) * head_masks[h], av[i][:bt]], axis=1))
        for i, (p, h) in enumerate(vheads)]
  x_p = [xy[2 * p][:, :LANES] + xy[2 * p + 1][:, :LANES] for p in range(n_pairs)]
  y_p = [xy[2 * p][:, LANES:] + xy[2 * p + 1][:, LANES:] for p in range(n_pairs)]
  akrv_p = [av[2 * p][bt:] + av[2 * p + 1][bt:] for p in range(n_pairs)]

  s = [s_ref[p] for p in range(n_pairs)]
  outs = [[] for _ in range(n_pairs)]
  for ci in range(n_chunks):
    sl = slice(ci * n, (ci + 1) * n)
    g_last = [pair(g_in, p)[(ci + 1) * n - 1:(ci + 1) * n] for p in range(n_pairs)]
    reads = [_dot_nt(jnp.concatenate([x_p[p][sl], pair(rt_all, p)[sl]], axis=0), s[p])
             for p in range(n_pairs)]
    u = [reads[p][:n] + y_p[p][sl] for p in range(n_pairs)]
    for p in range(n_pairs):
      u_rows = jnp.concatenate(
          [_rows_of_chunk(u[p] * chunk_head_masks[h], ci, bt) for h in range(2)], axis=0)
      a_br_rows = jnp.concatenate([a_br[2 * p][sl], a_br[2 * p + 1][sl]], axis=1)
      outs[p].append(reads[p][n:] + akrv_p[p][sl] - _dot(a_br_rows, u_rows))
      delta = _dot_tn(jnp.concatenate([pair(v, p)[sl], u[p]], axis=0),
                      jnp.concatenate([pair(kh_all, p)[sl] * g_last[p],
                                       -pair(bh_all, p)[sl] * g_last[p]], axis=0))
      s[p] = s[p] * g_last[p] + jnp.where(pair_diag, delta, 0.0)
  for p in range(n_pairs):
    s_ref[p] = s[p]
  o = jnp.concatenate([jnp.concatenate(outs[p], axis=0) for p in range(n_pairs)], axis=1)

  inv_n = 1.0 / RW_HEAD
  mean = _dot(o, head_ones) * inv_n
  d = o - mean
  var = _dot(d * d, head_ones) * inv_n
  o = d * lax.rsqrt(var + RW_GN_EPS) * lng_ref[...] + lnb_ref[...] + bonus
  o_ref[...] = (o * _silu(z_ref[...])).astype(o_ref.dtype)


RW_PAIRS_PER_STEP = 4


def rwkv7_mixer(y, mu, w0, w_up, a0, a_up, k_k, k_a, r_k, ln_g, ln_b, *, bt):
  t = y.shape[0]
  width = RW_PAIRS_PER_STEP * LANES
  groups = W_MIX // width
  col = lambda name: pl.BlockSpec((bt, width), lambda g, i, c=_COL[name] // width: (i, c + g))
  vec = lambda: pl.BlockSpec((1, width), lambda g, i: (0, g))
  zeros = jnp.zeros((RW_LORA, W_MIX), F32)
  wup_p = jnp.concatenate([w_up, zeros], axis=0)
  aup_p = jnp.concatenate([zeros, a_up], axis=0)
  row = lambda p: p.reshape(1, -1)
  return pl.pallas_call(
      _rwkv_kernel,
      grid=(groups, t // bt),
      in_specs=[
          col("A_r"), col("A_k"), col("A_v"),
          pl.BlockSpec((bt, LANES), lambda g, i, c=_COL["A_lo"] // LANES: (i, c)),
          col("A_z"),
          vec(), vec(), vec(), pl.BlockSpec((1, LANES), lambda g, i: (0, 0)),
          vec(), vec(), vec(), vec(), vec(), vec(), vec(),
          pl.BlockSpec((LANES, width), lambda g, i: (0, g)),
          pl.BlockSpec((LANES, width), lambda g, i: (0, g)),
      ],
      out_specs=pl.BlockSpec((bt, width), lambda g, i: (i, g)),
      out_shape=jax.ShapeDtypeStruct((t, W_MIX), BF16),
      scratch_shapes=[pltpu.VMEM((RW_PAIRS_PER_STEP, LANES, LANES), F32),
                      pltpu.VMEM((3, SUBLANES, width), F32),
                      pltpu.VMEM((1, SUBLANES, LANES), F32)],
      compiler_params=pltpu.CompilerParams(
          dimension_semantics=("parallel", "arbitrary"), vmem_limit_bytes=VMEM_LIMIT),
      name="rwkv7_mixer",
  )(y, y, y, y, y, row(mu[:W_MIX]), row(mu[W_MIX:2 * W_MIX]), row(mu[2 * W_MIX:3 * W_MIX]),
    row(mu[3 * W_MIX:]), row(w0), row(a0), row(k_k), row(k_a), row(r_k), row(ln_g), row(ln_b),
    wup_p, aup_p)


def _gdn_kernel(q_ref, k_ref, v_ref, g_ref, z_ref, cq_ref, ck_ref, cv_ref, alog_ref, dtb_ref,
                ng_ref, o_ref, s_ref, prev_ref):
  bt, width = q_ref.shape
  n = CHUNK
  n_heads = width // LANES
  n_chunks = bt // n
  first_head = pl.program_id(0) * n_heads

  @pl.when(pl.program_id(1) == 0)
  def _():
    s_ref[...] = jnp.zeros_like(s_ref)
    prev_ref[...] = jnp.zeros_like(prev_ref)

  def l2n(x):
    return x * lax.rsqrt(jnp.sum(x * x, axis=-1, keepdims=True) + L2_EPS)

  def head(x, j):
    return x[:, j * LANES:(j + 1) * LANES]

  q_all = _causal_conv_silu(q_ref, cq_ref, prev_ref, 0)
  k_all = _causal_conv_silu(k_ref, ck_ref, prev_ref, 1)
  v_all = _causal_conv_silu(v_ref, cv_ref, prev_ref, 2)
  gates = g_ref[...]
  beta_all = _sigmoid(gates)
  log_a_all = -jnp.exp(alog_ref[...]) * _softplus(gates + dtb_ref[...])
  strict, incl = _chunk_masks(bt)
  g_all = _mask_dot(jnp.where(incl, 1.0, 0.0), log_a_all, 2)
  g_all_t = g_all.T

  qs, ks, vs, kbs, g_cols, decays = [], [], [], [], [], []
  for j in range(n_heads):
    beta = _select_lane(beta_all, first_head + j)
    g_col = _select_lane(g_all, GD_H + first_head + j)
    g_row = _select_row(g_all_t, GD_H + first_head + j)
    decays.append(jnp.exp(jnp.minimum(g_col - g_row, 0.0)))
    k = l2n(head(k_all, j))
    qs.append(l2n(head(q_all, j)) * (GD_HD ** -0.5))
    ks.append(k)
    kbs.append(k * beta)
    vs.append(head(v_all, j) * beta)
    g_cols.append(g_col)
  heads = range(n_heads)
  big = [_dot_nt(jnp.concatenate([kbs[j], qs[j]], axis=0), ks[j]) for j in heads]
  m = [jnp.where(strict, big[j][:bt] * decays[j], 0.0) for j in heads]
  attn = [jnp.where(incl, big[j][bt:] * decays[j], 0.0) for j in heads]
  t_inv = _inv_unit_lower(m)
  e_g = [jnp.exp(g) for g in g_cols]
  yx = [_dot(t_inv[j], jnp.concatenate([vs[j], kbs[j] * e_g[j]], axis=1)) for j in heads]
  qe = [qs[j] * e_g[j] for j in heads]

  s = [s_ref[j] for j in heads]
  outs = [[] for _ in heads]
  for ci in range(n_chunks):
    sl = slice(ci * n, (ci + 1) * n)
    reads = [_dot(jnp.concatenate([yx[j][sl, LANES:], qe[j][sl]], axis=0), s[j]) for j in heads]
    for j in heads:
      v_new = yx[j][sl, :LANES] - reads[j][:n]
      outs[j].append(reads[j][n:] + _dot(attn[j][sl], _rows_of_chunk(v_new, ci, bt)))
      g_end = g_cols[j][(ci + 1) * n - 1:(ci + 1) * n]
      s[j] = jnp.exp(g_end) * s[j] + _dot_tn(ks[j][sl] * jnp.exp(g_end - g_cols[j][sl]), v_new)
  for j in heads:
    s_ref[j] = s[j]
  o = jnp.concatenate(
      [_head_rms(jnp.concatenate(outs[j], axis=0), head(ng_ref[...], j)) for j in heads], axis=1)
  o_ref[...] = (o * _silu(z_ref[...])).astype(o_ref.dtype)


GD_HEADS_PER_STEP = 4


def gdn_mixer(y, conv_w, a_log, dt_bias, norm_g, *, bt):
  t = y.shape[0]
  width = GD_HEADS_PER_STEP * LANES
  groups = W_MIX // width
  col = lambda name: pl.BlockSpec((bt, width), lambda g, i, c=_COL[name] // width: (i, c + g))
  cw = lambda part: pl.BlockSpec((CONV_K, width), lambda g, i, p=part: (0, p * groups + g))
  fix = pl.BlockSpec((1, LANES), lambda g, i: (0, 0))
  on_gate_lanes = lambda p: jnp.zeros((1, LANES), F32).at[0, GD_H:2 * GD_H].set(p)
  return pl.pallas_call(
      _gdn_kernel,
      grid=(groups, t // bt),
      in_specs=[
          col("D_q"), col("D_k"), col("D_v"),
          pl.BlockSpec((bt, LANES), lambda g, i, c=_COL["D_g"] // LANES: (i, c)),
          col("D_z"), cw(0), cw(1), cw(2), fix, fix,
          pl.BlockSpec((1, width), lambda g, i: (0, g)),
      ],
      out_specs=pl.BlockSpec((bt, width), lambda g, i: (i, g)),
      out_shape=jax.ShapeDtypeStruct((t, W_MIX), BF16),
      scratch_shapes=[pltpu.VMEM((GD_HEADS_PER_STEP, GD_HD, GD_HD), F32),
                      pltpu.VMEM((3, SUBLANES, width), F32)],
      compiler_params=pltpu.CompilerParams(
          dimension_semantics=("parallel", "arbitrary"), vmem_limit_bytes=VMEM_LIMIT),
      name="gdn_mixer",
  )(y, y, y, y, y, conv_w, conv_w, conv_w, on_gate_lanes(a_log), on_gate_lanes(dt_bias),
    norm_g.reshape(1, -1))


def _mlstm_kernel(q_ref, k_ref, v_ref, og_ref, z_ref, g_ref, cq_ref, ck_ref, ib_ref, fb_ref,
                  ng_ref, o_ref, c_ref, n_ref, m_ref, prev_ref):
  bt = q_ref.shape[0]
  n = CHUNK
  h = pl.program_id(0)

  @pl.when(pl.program_id(1) == 0)
  def _():
    c_ref[...] = jnp.zeros_like(c_ref)
    n_ref[...] = jnp.zeros_like(n_ref)
    m_ref[...] = jnp.zeros_like(m_ref)
    prev_ref[...] = jnp.zeros_like(prev_ref)

  q = _causal_conv_silu(q_ref, cq_ref, prev_ref, 0)
  k = _causal_conv_silu(k_ref, ck_ref, prev_ref, 1) * (ML_HD ** -0.5)
  v = v_ref[...]
  gates = g_ref[...]
  log_i = _select_lane(gates, h) + ib_ref[...]
  log_f = -_softplus(-(_select_lane(gates, ML_H + h) + fb_ref[...]))

  tr, tc = _tri(n)
  causal = tc <= tr
  lower_incl = jnp.where(causal, 1.0, 0.0)
  upper_incl = jnp.where(tr <= tc, 1.0, 0.0)
  eye = jnp.where(tr == tc, 1.0, 0.0)
  ones = jnp.ones((n, n), F32)

  c_st, n_st, m_st = c_ref[...], n_ref[...], m_ref[...]
  outs = []
  for ci in range(bt // n):
    sl = slice(ci * n, (ci + 1) * n)
    fc, ic = log_f[sl], log_i[sl]
    g_col = _mask_dot(lower_incl, fc)
    g_row = _mask_dot(ones, upper_incl * fc[:, :n])
    i_row = _mask_dot(ones, eye * ic[:, :n])
    dmat = jnp.where(causal, g_col[:, :n] - g_row + i_row, -jnp.inf)
    inter = g_col + m_st
    m_row = jnp.maximum(jnp.max(dmat, axis=-1, keepdims=True), inter)
    qc, kc, vc = q[sl], k[sl], v[sl]
    sc = _dot_nt(qc, kc) * jnp.exp(dmat - m_row[:, :n])
    w_inter = jnp.exp(inter - m_row)[:, :1]
    num = _dot(sc, vc) + w_inter * _dot(qc, c_st)
    den = (jnp.sum(sc, axis=-1, keepdims=True)
           + w_inter * jnp.sum(qc * n_st, axis=-1, keepdims=True))
    outs.append(num / jnp.maximum(jnp.abs(den), jnp.exp(-m_row[:, :1])))
    g_end = g_col[n - 1:n]
    log_w = g_end - g_col + ic
    m_new = jnp.maximum(g_end + m_st, jnp.max(log_w, axis=0, keepdims=True))
    carry = jnp.exp(g_end + m_st - m_new)[:, :1]
    wk = jnp.exp(log_w - m_new)[:, :1] * kc
    c_st = carry * c_st + _dot_tn(wk, vc)
    n_st = carry * n_st + jnp.sum(wk, axis=0, keepdims=True)
    m_st = m_new
  c_ref[...] = c_st
  n_ref[...] = n_st
  m_ref[...] = m_st
  hh = _head_rms(jnp.concatenate(outs, axis=0), ng_ref[...]) * _sigmoid(og_ref[...])
  o_ref[...] = (hh * _silu(z_ref[...])).astype(o_ref.dtype)


def mlstm_mixer(y, conv_w, i_bias, f_bias, norm_g, *, bt):
  t = y.shape[0]
  cb = lambda name: _COL[name] // ML_HD
  col = lambda name: pl.BlockSpec((bt, ML_HD), lambda h, i, c=cb(name): (i, c + h))
  cw = lambda part: pl.BlockSpec((CONV_K, ML_HD), lambda h, i, p=part: (0, p * ML_H + h))
  scal = lambda: pl.BlockSpec((None, 1, LANES), lambda h, i: (h, 0, 0))
  bcast = lambda p: jnp.broadcast_to(p[:, None, None], (ML_H, 1, LANES))
  return pl.pallas_call(
      _mlstm_kernel,
      grid=(ML_H, t // bt),
      in_specs=[
          col("B_q"), col("B_k"), col("B_v"), col("B_o"), col("B_z"),
          pl.BlockSpec((bt, LANES), lambda h, i, c=_COL["B_g"] // LANES: (i, c)),
          cw(0), cw(1), scal(), scal(),
          pl.BlockSpec((1, ML_HD), lambda h, i: (0, h)),
      ],
      out_specs=pl.BlockSpec((bt, ML_HD), lambda h, i: (i, h)),
      out_shape=jax.ShapeDtypeStruct((t, W_MIX), BF16),
      scratch_shapes=[pltpu.VMEM((ML_HD, ML_HD), F32), pltpu.VMEM((1, ML_HD), F32),
                      pltpu.VMEM((1, LANES), F32), pltpu.VMEM((2, SUBLANES, ML_HD), F32)],
      compiler_params=pltpu.CompilerParams(
          dimension_semantics=("parallel", "arbitrary"), vmem_limit_bytes=VMEM_LIMIT),
      name="mlstm_mixer",
  )(y, y, y, y, y, y, conv_w, conv_w, bcast(i_bias), bcast(f_bias), norm_g.reshape(1, -1))


HG_SUB = 16


def _hgrn2_kernel(q_ref, f_ref, i_ref, z_ref, lb_ref, ng_ref, o_ref, st_ref):
  bt = q_ref.shape[0]
  n = CHUNK
  n_chunks = bt // n

  @pl.when(pl.program_id(1) == 0)
  def _():
    st_ref[...] = jnp.zeros_like(st_ref)

  lb = lb_ref[...]
  f = lb + (1.0 - lb) * _sigmoid(f_ref[...])
  q = _silu(q_ref[...])
  k = 1.0 - f
  log_f = jnp.log(f)
  v = i_ref[...]

  def group_row(x, size, r):
    x3 = x.reshape(bt // size, size, HG_HD)
    return jnp.broadcast_to(x3[:, r:r + 1, :], x3.shape).reshape(bt, HG_HD)

  strict, incl = _chunk_masks(bt)
  b = _mask_dot(jnp.where(incl, 1.0, 0.0), log_f, 2)
  b_ex = b - log_f
  row = _iota((bt, HG_HD), 0)
  row_in_block = row & (HG_SUB - 1)
  row_in_chunk = row & (n - 1)

  o = jnp.zeros((bt, HG_HD), F32)
  for l in range(HG_SUB):
    e = jnp.exp(jnp.where(row_in_block >= l, b - group_row(b, HG_SUB, l), -jnp.inf))
    o = o + (jnp.sum(q * group_row(k, HG_SUB, l) * e, axis=-1, keepdims=True)
             * group_row(v, HG_SUB, l))

  qs, ks = [], []
  for bi in range(1, n // HG_SUB):
    start = bi * HG_SUB
    ref = group_row(b_ex, n, start)
    in_block = (row_in_chunk >= start) & (row_in_chunk < start + HG_SUB)
    qs.append(q * jnp.exp(jnp.where(in_block, b - ref, -jnp.inf)))
    ks.append(k * jnp.exp(jnp.where(row_in_chunk < start, ref - b, -jnp.inf)))
  a_off = _dot_nt(jnp.concatenate(qs, axis=1), jnp.concatenate(ks, axis=1))
  o = o + _dot(jnp.where(strict, a_off, 0.0), v)

  qe = q * jnp.exp(b)
  kd = k * jnp.exp(group_row(b, n, n - 1) - b)
  st = st_ref[...]
  reads = []
  for ci in range(n_chunks):
    sl = slice(ci * n, (ci + 1) * n)
    reads.append(_dot_nt(qe[sl], st))
    st = st * jnp.exp(b[(ci + 1) * n - 1:(ci + 1) * n]) + _dot_tn(v[sl], kd[sl])
  st_ref[...] = st
  o = o + jnp.concatenate(reads, axis=0)
  o_ref[...] = (_head_rms(o, ng_ref[...]) * _silu(z_ref[...])).astype(o_ref.dtype)


def hgrn2_mixer(y, lower_bound, norm_g, *, bt):
  t = y.shape[0]
  cb = lambda name: _COL[name] // LANES
  col = lambda name: pl.BlockSpec((bt, LANES), lambda h, i, c=cb(name): (i, c + h))
  vec = lambda: pl.BlockSpec((1, LANES), lambda h, i: (0, h))
  return pl.pallas_call(
      _hgrn2_kernel,
      grid=(HG_H, t // bt),
      in_specs=[col("C_q"), col("C_f"), col("C_i"), col("C_z"), vec(), vec()],
      out_specs=pl.BlockSpec((bt, LANES), lambda h, i: (i, h)),
      out_shape=jax.ShapeDtypeStruct((t, W_MIX), BF16),
      scratch_shapes=[pltpu.VMEM((HG_HD, HG_HD), F32)],
      compiler_params=pltpu.CompilerParams(
          dimension_semantics=("parallel", "arbitrary"), vmem_limit_bytes=VMEM_LIMIT),
      name="hgrn2_mixer",
  )(y, y, y, y, lower_bound.reshape(1, -1), norm_g.reshape(1, -1))


def _lower_bounds_kernel(x_ref, o_ref):
  x = x_ref[...]
  e = jnp.exp(x - jnp.max(x, axis=0, keepdims=True))
  p = e / jnp.sum(e, axis=0, keepdims=True)
  acc = jnp.zeros_like(p[0:1])
  for l in range(DEPTH):
    o_ref[l:l + 1, :] = acc
    if l + 1 < DEPTH:
      acc = acc + p[l + 1:l + 2]


def hgrn2_lower_bounds(hg_lower):
  return pl.pallas_call(
      _lower_bounds_kernel,
      out_shape=jax.ShapeDtypeStruct(hg_lower.shape, F32),
      name="hgrn2_lower_bounds",
  )(hg_lower)


def _ada_kernel(c_ref, w_ref, b_ref, o_ref):
  @pl.when(pl.program_id(1) == 0)
  def _():
    o_ref[...] = b_ref[...]

  ca = _silu(c_ref[...])
  n_out = w_ref.shape[1]
  cols = [jnp.sum(w_ref[:, j * LANES:(j + 1) * LANES] * ca, axis=0, keepdims=True)
          for j in range(n_out // LANES)]
  o_ref[...] += jnp.concatenate(cols, axis=1)


def adaln_params(c, w_ada, b_ada, *, tk=128):
  depth, d, n_out = w_ada.shape
  c_cols = jnp.broadcast_to(c.reshape(d, 1), (d, LANES))
  return pl.pallas_call(
      _ada_kernel,
      grid=(depth, d // tk),
      in_specs=[
          pl.BlockSpec((tk, LANES), lambda l, k: (k, 0)),
          pl.BlockSpec((None, tk, n_out), lambda l, k: (l, k, 0)),
          pl.BlockSpec((None, 1, n_out), lambda l, k: (l, 0, 0)),
      ],
      out_specs=pl.BlockSpec((None, 1, n_out), lambda l, k: (l, 0, 0)),
      out_shape=jax.ShapeDtypeStruct((depth, 1, n_out), F32),
      compiler_params=pltpu.CompilerParams(
          dimension_semantics=("parallel", "arbitrary"), vmem_limit_bytes=VMEM_LIMIT),
      name="adaln_params",
  )(c_cols, w_ada, b_ada.reshape(depth, 1, n_out))


def _norm_mod_kernel(x_ref, g_ref, shift_ref, scale_ref, o_ref):
  x = x_ref[...]
  y = x * lax.rsqrt(jnp.mean(x * x, axis=-1, keepdims=True) + NORM_EPS) * g_ref[...]
  o_ref[...] = (y * (1.0 + scale_ref[...]) + shift_ref[...]).astype(o_ref.dtype)


def norm_modulate(x, g, shift, scale, *, tm=256):
  t, d = x.shape
  vec = pl.BlockSpec((1, d), lambda i: (0, 0))
  return pl.pallas_call(
      _norm_mod_kernel,
      grid=(t // tm,),
      in_specs=[pl.BlockSpec((tm, d), lambda i: (i, 0)), vec, vec, vec],
      out_specs=pl.BlockSpec((tm, d), lambda i: (i, 0)),
      out_shape=jax.ShapeDtypeStruct((t, d), BF16),
      compiler_params=pltpu.CompilerParams(
          dimension_semantics=("parallel",), vmem_limit_bytes=VMEM_LIMIT),
      name="norm_modulate",
  )(x, g.reshape(1, d), shift, scale)


def _final_norm_kernel(x_ref, g_ref, o_ref):
  x = x_ref[...]
  o_ref[...] = x * lax.rsqrt(jnp.mean(x * x, axis=-1, keepdims=True) + NORM_EPS) * g_ref[...]


def final_norm(x, g, *, tm=256):
  t, d = x.shape
  return pl.pallas_call(
      _final_norm_kernel,
      grid=(t // tm,),
      in_specs=[pl.BlockSpec((tm, d), lambda i: (i, 0)), pl.BlockSpec((1, d), lambda i: (0, 0))],
      out_specs=pl.BlockSpec((tm, d), lambda i: (i, 0)),
      out_shape=jax.ShapeDtypeStruct((t, d), F32),
      compiler_params=pltpu.CompilerParams(
          dimension_semantics=("parallel",), vmem_limit_bytes=VMEM_LIMIT),
      name="final_norm",
  )(x, g.reshape(1, d))


def _in_proj_kernel(h_ref, w_ref, o_ref):
  o_ref[...] = jnp.dot(h_ref[...], w_ref[...], preferred_element_type=F32)


def in_proj(h, w_all, layer, *, tm=512, tn=1280):
  t, d = h.shape
  n_out = w_all.shape[2]
  return pl.pallas_call(
      _in_proj_kernel,
      grid=(n_out // tn, t // tm),
      in_specs=[pl.BlockSpec((tm, d), lambda j, i: (i, 0)),
                pl.BlockSpec((None, d, tn), lambda j, i: (layer, 0, j))],
      out_specs=pl.BlockSpec((tm, tn), lambda j, i: (i, j)),
      out_shape=jax.ShapeDtypeStruct((t, n_out), F32),
      compiler_params=pltpu.CompilerParams(
          dimension_semantics=("parallel", "parallel"), vmem_limit_bytes=VMEM_LIMIT),
      name="in_proj",
  )(h, w_all)


def _out_proj_kernel(oa_ref, ob_ref, oc_ref, od_ref, w_ref, x_ref, gate_ref, o_ref, wb_ref):
  @pl.when(pl.program_id(1) == 0)
  def _():
    wb_ref[...] = w_ref[...].astype(BF16)

  acc = jnp.dot(oa_ref[...], wb_ref[0:W_MIX, :], preferred_element_type=F32)
  for g, ref in enumerate((ob_ref, oc_ref, od_ref), start=1):
    acc = acc + jnp.dot(ref[...], wb_ref[g * W_MIX:(g + 1) * W_MIX, :],
                        preferred_element_type=F32)
  o_ref[...] = x_ref[...] + gate_ref[...] * acc


def out_proj_residual(o_groups, w_all, layer, x, gate, *, tm=512, tn=1024):
  t, d = x.shape
  og = pl.BlockSpec((tm, W_MIX), lambda j, i: (i, 0))
  return pl.pallas_call(
      _out_proj_kernel,
      grid=(d // tn, t // tm),
      in_specs=[og, og, og, og,
                pl.BlockSpec((None, 4 * W_MIX, tn), lambda j, i: (layer, 0, j),
                             pipeline_mode=pl.Buffered(1)),
                pl.BlockSpec((tm, tn), lambda j, i: (i, j)),
                pl.BlockSpec((1, tn), lambda j, i: (0, j))],
      out_specs=pl.BlockSpec((tm, tn), lambda j, i: (i, j)),
      out_shape=jax.ShapeDtypeStruct((t, d), F32),
      scratch_shapes=[pltpu.VMEM((4 * W_MIX, tn), BF16)],
      compiler_params=pltpu.CompilerParams(
          dimension_semantics=("parallel", "arbitrary"), vmem_limit_bytes=VMEM_LIMIT),
      name="out_proj_residual",
  )(*o_groups, w_all, x, gate)


MIXER_BT = 256


def kernel(x, c, norm_g, w_ada, b_ada, w_in, w_out, rw_mu, rw_w0, rw_w_up, rw_a0, rw_a_up, rw_k_k,
           rw_k_a, rw_r_k, rw_ln_g, rw_ln_b, ml_conv, ml_i_bias, ml_f_bias, ml_norm_g, hg_lower,
           hg_norm_g, gd_conv, gd_A_log, gd_dt_bias, gd_norm_g, final_g):
  bsz, seq, d = x.shape
  assert bsz == 1 and d == D_MODEL
  xs = x.reshape(seq, d)
  lower_bounds = hgrn2_lower_bounds(hg_lower)
  ada = adaln_params(c, w_ada, b_ada)
  w_in_p = relayout_in_proj_weights(w_in)
  bt = MIXER_BT
  for l in range(DEPTH):
    shift, scale, gate = ada[l, :, :d], ada[l, :, d:2 * d], ada[l, :, 2 * d:]
    h = norm_modulate(xs, norm_g[l], shift, scale)
    y = in_proj(h, w_in_p, l)
    o_a = rwkv7_mixer(y, rw_mu[l], rw_w0[l], rw_w_up[l], rw_a0[l], rw_a_up[l], rw_k_k[l],
                      rw_k_a[l], rw_r_k[l], rw_ln_g[l], rw_ln_b[l], bt=bt)
    o_b = mlstm_mixer(y, ml_conv[l], ml_i_bias[l], ml_f_bias[l], ml_norm_g[l], bt=2 * bt)
    o_c = hgrn2_mixer(y, lower_bounds[l], hg_norm_g[l], bt=bt)
    o_d = gdn_mixer(y, gd_conv[l], gd_A_log[l], gd_dt_bias[l], gd_norm_g[l], bt=bt)
    xs = out_proj_residual((o_a, o_b, o_c, o_d), w_out, l, xs, gate)
  return final_norm(xs, final_g).reshape(bsz, seq, d)
```

```python
import functools

import jax
import jax.numpy as jnp
from jax import lax
from jax.experimental import pallas as pl
from jax.experimental.pallas import tpu as pltpu

F32 = jnp.float32
BF16 = jnp.bfloat16

D_MODEL = 4096
DEPTH = 4
W_MIX = 1024
RW_HEAD = 64
RW_LORA = 64
RW_DECAY_SCALE = 0.606531
RW_GN_EPS = 64e-5
ML_H = 4
ML_HD = 256
HG_H = 8
HG_HD = 128
GD_H = 8
GD_HD = 128
CONV_K = 4
CHUNK = 64
_LOG2_CHUNK = 6
_LOG2_RW_HEAD = 6
assert 1 << _LOG2_CHUNK == CHUNK and 1 << _LOG2_RW_HEAD == RW_HEAD
NORM_EPS = 1e-6
L2_EPS = 1e-6

LANES = 128
SUBLANES = 8
VMEM_LIMIT = 48 * 1024 * 1024

_COL = dict(
    B_q=0, B_k=1024, B_v=2048, B_o=3072, B_z=4096,
    C_q=5120, C_f=6144, C_i=7168, C_z=8192,
    A_r=9216, A_k=10240, A_v=11264, A_z=12288,
    D_q=13312, D_k=14336, D_v=15360, D_z=16384,
    A_lo=17408, B_g=17536, D_g=17664,
)
P_PAD = 17920
_A0, _B0, _C0, _D0 = 0, 4224, 9352, 13448
_SRC = (
    ("B_q", _B0, 1024), ("B_k", _B0 + 1024, 1024), ("B_v", _B0 + 2048, 1024),
    ("B_o", _B0 + 3072, 1024), ("B_z", _B0 + 4104, 1024),
    ("C_q", _C0, 1024), ("C_f", _C0 + 1024, 1024), ("C_i", _C0 + 2048, 1024),
    ("C_z", _C0 + 3072, 1024),
    ("A_r", _A0, 1024), ("A_k", _A0 + 1024, 1024), ("A_v", _A0 + 2048, 1024),
    ("A_z", _A0 + 3200, 1024),
    ("D_q", _D0, 1024), ("D_k", _D0 + 1024, 1024), ("D_v", _D0 + 2048, 1024),
    ("D_z", _D0 + 3088, 1024),
    ("A_lo", _A0 + 3072, 128), ("B_g", _B0 + 4096, 8), ("D_g", _D0 + 3072, 16),
)


def relayout_columns(w):
  parts, pos = [], 0
  for name, src, width in _SRC:
    assert _COL[name] == pos, (name, pos)
    parts.append(w[..., src:src + width])
    pad = (-width) % LANES
    if pad:
      parts.append(jnp.zeros(w.shape[:-1] + (pad,), w.dtype))
    pos += width + pad
  parts.append(jnp.zeros(w.shape[:-1] + (P_PAD - pos,), w.dtype))
  return jnp.concatenate(parts, axis=-1)


RELAY_TILE = 512
RELAY_K = 4096
_N_BIG = 17
_N_BIG_TILES = _N_BIG * (1024 // RELAY_TILE)
assert _COL["A_lo"] == _N_BIG_TILES * RELAY_TILE and P_PAD == (_N_BIG_TILES + 1) * RELAY_TILE


def _relayout_kernel(src_ref, w_ref, tail_ref, o_ref):
  t = pl.program_id(1)

  @pl.when(t < _N_BIG_TILES)
  def _():
    o_ref[...] = w_ref[...].T.astype(o_ref.dtype)

  @pl.when(t >= _N_BIG_TILES)
  def _():
    o_ref[...] = tail_ref[...].T.astype(o_ref.dtype)


def relayout_in_proj_weights(w_in):
  depth, d, _ = w_in.shape
  wt = jnp.swapaxes(w_in, 1, 2)
  starts = [src + half * RELAY_TILE for _, src, _ in _SRC[:_N_BIG]
            for half in range(1024 // RELAY_TILE)]
  assert all(s % SUBLANES == 0 for s in starts)
  src_rows8 = jnp.asarray([s // SUBLANES for s in starts], jnp.int32)
  parts, pos = [], 0
  for _, src, width in _SRC[_N_BIG:]:
    parts.append(wt[:, src:src + width])
    pad = (-width) % LANES
    if pad:
      parts.append(jnp.zeros((depth, pad, d), wt.dtype))
    pos += width + pad
  parts.append(jnp.zeros((depth, RELAY_TILE - pos, d), wt.dtype))
  tail = jnp.concatenate(parts, axis=1)
  return pl.pallas_call(
      _relayout_kernel,
      grid_spec=pltpu.PrefetchScalarGridSpec(
          num_scalar_prefetch=1,
          grid=(depth, _N_BIG_TILES + 1, d // RELAY_K),
          in_specs=[
              pl.BlockSpec((None, pl.Element(RELAY_TILE), pl.Element(RELAY_K)),
                           lambda l, t, k, src: (
                               l, src[jnp.minimum(t, _N_BIG_TILES - 1)] * SUBLANES, k * RELAY_K)),
              pl.BlockSpec((None, RELAY_TILE, RELAY_K), lambda l, t, k, src: (l, 0, k)),
          ],
          out_specs=pl.BlockSpec((None, RELAY_K, RELAY_TILE), lambda l, t, k, src: (l, k, t))),
      out_shape=jax.ShapeDtypeStruct((depth, d, P_PAD), BF16),
      compiler_params=pltpu.CompilerParams(
          dimension_semantics=("parallel", "parallel", "parallel"),
          vmem_limit_bytes=VMEM_LIMIT),
      name="relayout_in_proj_weights",
  )(src_rows8, wt, tail)


def _iota(shape, dim):
  return lax.broadcasted_iota(jnp.int32, shape, dim)


def _dot(a, b):
  return jnp.dot(a.astype(BF16), b.astype(BF16), preferred_element_type=F32)


def _dot_nt(a, b):
  return lax.dot_general(a.astype(BF16), b.astype(BF16), (((1,), (1,)), ((), ())),
                         preferred_element_type=F32)


def _dot_tn(a, b):
  return lax.dot_general(a.astype(BF16), b.astype(BF16), (((0,), (0,)), ((), ())),
                         preferred_element_type=F32)


def _bf16_terms(x, n_terms=3):
  terms, rest = [], x
  for _ in range(n_terms):
    t = rest.astype(BF16)
    terms.append(t)
    rest = rest - t.astype(F32)
  return terms


def _mask_dot(mask, x, n_terms=3):
  m = mask.astype(BF16)
  return functools.reduce(
      jnp.add, [jnp.dot(m, t, preferred_element_type=F32) for t in _bf16_terms(x, n_terms)])


def _sigmoid(x):
  return 1.0 / (1.0 + jnp.exp(-x))


def _silu(x):
  return x * _sigmoid(x)


def _softplus(x):
  return jnp.maximum(x, 0.0) + jnp.log1p(jnp.exp(-jnp.abs(x)))


def _shift_rows(x, prev8, j):
  rolled = pltpu.roll(x, j, 0)
  top = jnp.where(_iota((SUBLANES, x.shape[1]), 0) < j, pltpu.roll(prev8, j, 0),
                  rolled[:SUBLANES])
  return jnp.concatenate([top, rolled[SUBLANES:]], axis=0)


def _causal_conv_silu(x_ref, w_ref, prev_ref, slot):
  x = x_ref[...]
  prev8 = prev_ref[slot]
  acc = x * w_ref[CONV_K - 1:CONV_K, :]
  for j in range(1, CONV_K):
    acc = acc + _shift_rows(x, prev8, j) * w_ref[CONV_K - 1 - j:CONV_K - j, :]
  prev_ref[slot] = x[x.shape[0] - SUBLANES:]
  return _silu(acc)


def _tri(n):
  r, c = _iota((n, n), 0), _iota((n, n), 1)
  return r, c


def _chunk_masks(rows):
  r, c = _tri(rows)
  same = (r >> _LOG2_CHUNK) == (c >> _LOG2_CHUNK)
  return same & (c < r), same & (c <= r)


def _inv_unit_lower(mats):
  n = mats[0].shape[0]
  r, c = _tri(n)
  same = lambda log2_size: (r >> log2_size) == (c >> log2_size)
  base = 3
  mm = lambda x, y: jnp.dot(x.astype(BF16), y.astype(BF16), preferred_element_type=F32)
  diag = [jnp.where(same(base), a, 0.0).astype(BF16) for a in mats]
  xs = [jnp.where(r == c, 1.0, 0.0) - d for d in diag]
  ps = [mm(d, d) for d in diag]
  xs = [x + mm(x, p) for x, p in zip(xs, ps)]
  ps = [mm(p, p) for p in ps]
  xs = [x + mm(x, p) for x, p in zip(xs, ps)]
  for level in range(base, _LOG2_CHUNK):
    sibling = same(level + 1) & jnp.logical_not(same(level))
    offs = [jnp.where(sibling, a, 0.0) for a in mats]
    half = [mm(x, o) for x, o in zip(xs, offs)]
    xs = [x - mm(h, x) for x, h in zip(xs, half)]
  return xs


def _select_lane(x, lane_idx):
  return jnp.sum(jnp.where(_iota(x.shape, 1) == lane_idx, x, 0.0), axis=-1, keepdims=True)


def _select_row(x, row_idx):
  return jnp.sum(jnp.where(_iota(x.shape, 0) == row_idx, x, 0.0), axis=0, keepdims=True)


def _rows_of_chunk(x, ci, total_rows):
  parts = []
  if ci:
    parts.append(jnp.zeros((ci * CHUNK, x.shape[1]), x.dtype))
  parts.append(x)
  rest = total_rows - (ci + 1) * CHUNK
  if rest:
    parts.append(jnp.zeros((rest, x.shape[1]), x.dtype))
  return jnp.concatenate(parts, axis=0)


def _head_rms(o, g):
  return o * lax.rsqrt(jnp.mean(o * o, axis=-1, keepdims=True) + NORM_EPS) * g


def _rwkv_kernel(r_ref, k_ref, v_ref, lo_ref, z_ref, mur_ref, muk_ref, muv_ref, mulo_ref,
                 w0_ref, a0_ref, kkw_ref, kaw_ref, rkw_ref, lng_ref, lnb_ref, wup_ref, aup_ref,
                 o_ref, s_ref, prev_ref, prevlo_ref):
  bt, width = r_ref.shape
  n = CHUNK
  n_pairs = width // LANES
  n_chunks = bt // n

  @pl.when(pl.program_id(1) == 0)
  def _():
    s_ref[...] = jnp.zeros_like(s_ref)
    prev_ref[...] = jnp.zeros_like(prev_ref)
    prevlo_ref[...] = jnp.zeros_like(prevlo_ref)

  def token_shift(x_ref, mu_ref, carry_ref, slot):
    x = x_ref[...]
    prev = _shift_rows(x, carry_ref[slot], 1)
    carry_ref[slot] = x[bt - SUBLANES:]
    return x + (prev - x) * mu_ref[...]

  r = token_shift(r_ref, mur_ref, prev_ref, 0)
  k = token_shift(k_ref, muk_ref, prev_ref, 1)
  v = token_shift(v_ref, muv_ref, prev_ref, 2)
  lo = token_shift(lo_ref, mulo_ref, prevlo_ref, 0)

  lo_act = jnp.where(_iota((bt, LANES), 1) < RW_LORA, jnp.tanh(lo), lo)
  log_w = -RW_DECAY_SCALE * _sigmoid(w0_ref[...] + _dot(lo_act, wup_ref[...]))
  a = _sigmoid(a0_ref[...] + _dot(lo_act, aup_ref[...]))

  hr, hc = _tri(width)
  head_ones = jnp.where(hr >> _LOG2_RW_HEAD == hc >> _LOG2_RW_HEAD, 1.0, 0.0)
  kkp = k * kkw_ref[...]
  k2 = k * (1.0 + (a - 1.0) * kaw_ref[...])
  sums = _dot(jnp.concatenate([kkp * kkp, r * k2 * rkw_ref[...]], axis=0), head_ones)
  kk = kkp * lax.rsqrt(sums[:bt] + L2_EPS)
  b = kk * a
  bonus = sums[bt:] * v

  strict, incl = _chunk_masks(bt)
  lc = _mask_dot(jnp.where(incl, 1.0, 0.0), log_w, 2)
  g_in = jnp.exp(lc)
  g_inv = jnp.exp(-lc)
  kap_all = kk * jnp.exp(lc - log_w)
  rt_all = r * g_in
  kh_all = k2 * g_inv
  bh_all = b * g_inv

  lane = _iota((bt, LANES), 1)
  lane_n = _iota((n, LANES), 1)
  head_masks = [jnp.where(lane < RW_HEAD, 1.0, 0.0), jnp.where(lane >= RW_HEAD, 1.0, 0.0)]
  chunk_head_masks = [jnp.where(lane_n < RW_HEAD, 1.0, 0.0), jnp.where(lane_n >= RW_HEAD, 1.0, 0.0)]
  pr, pc = _tri(LANES)
  pair_diag = (pr >> _LOG2_RW_HEAD) == (pc >> _LOG2_RW_HEAD)

  def pair(x, p):
    return x[:, p * LANES:(p + 1) * LANES]

  vheads = [(p, h) for p in range(n_pairs) for h in range(2)]
  big = [_dot_nt(jnp.concatenate([pair(kap_all, p), pair(rt_all, p)], axis=0),
                 jnp.concatenate([pair(kh_all, p) * head_masks[h],
                                  pair(bh_all, p) * head_masks[h]], axis=0))
         for p, h in vheads]
  a_kk = [jnp.where(strict, m[:bt, :bt], 0.0) for m in big]
  a_bk = [jnp.where(strict, m[:bt, bt:], 0.0) for m in big]
  a_kr = [jnp.where(incl, m[bt:, :bt], 0.0) for m in big]
  a_br = [jnp.where(incl, m[bt:, bt:], 0.0) for m in big]
  t_inv = _inv_unit_lower(a_bk)
  av = [_dot(jnp.concatenate([a_kk[i], a_kr[i]], axis=0), pair(v, p) * head_masks[h])
        for i, (p, h) in enumerate(vheads)]
  xy = [_dot(t_inv[i], jnp.concatenate([pair(kap_all, p) * head_masks[h], av[i][:bt]], axis=1))
        for i, (p, h) in enumerate(vheads)]
  x_p = [xy[2 * p][:, :LANES] + xy[2 * p + 1][:, :LANES] for p in range(n_pairs)]
  y_p = [xy[2 * p][:, LANES:] + xy[2 * p + 1][:, LANES:] for p in range(n_pairs)]
  akrv_p = [av[2 * p][bt:] + av[2 * p + 1][bt:] for p in range(n_pairs)]

  s = [s_ref[p] for p in range(n_pairs)]
  outs = [[] for _ in range(n_pairs)]
  for ci in range(n_chunks):
    sl = slice(ci * n, (ci + 1) * n)
    g_last = [pair(g_in, p)[(ci + 1) * n - 1:(ci + 1) * n] for p in range(n_pairs)]
    reads = [_dot_nt(jnp.concatenate([x_p[p][sl], pair(rt_all, p)[sl]], axis=0), s[p])
             for p in range(n_pairs)]
    u = [reads[p][:n] + y_p[p][sl] for p in range(n_pairs)]
    for p in range(n_pairs):
      u_rows = jnp.concatenate(
          [_rows_of_chunk(u[p] * chunk_head_masks[h], ci, bt) for h in range(2)], axis=0)
      a_br_rows = jnp.concatenate([a_br[2 * p][sl], a_br[2 * p + 1][sl]], axis=1)
      outs[p].append(reads[p][n:] + akrv_p[p][sl] - _dot(a_br_rows, u_rows))
      delta = _dot_tn(jnp.concatenate([pair(v, p)[sl], u[p]], axis=0),
                      jnp.concatenate([pair(kh_all, p)[sl] * g_last[p],
                                       -pair(bh_all, p)[sl] * g_last[p]], axis=0))
      s[p] = s[p] * g_last[p] + jnp.where(pair_diag, delta, 0.0)
  for p in range(n_pairs):
    s_ref[p] = s[p]
  o = jnp.concatenate([jnp.concatenate(outs[p], axis=0) for p in range(n_pairs)], axis=1)

  inv_n = 1.0 / RW_HEAD
  mean = _dot(o, head_ones) * inv_n
  d = o - mean
  var = _dot(d * d, head_ones) * inv_n
  o = d * lax.rsqrt(var + RW_GN_EPS) * lng_ref[...] + lnb_ref[...] + bonus
  o_ref[...] = (o * _silu(z_ref[...])).astype(o_ref.dtype)


RW_PAIRS_PER_STEP = 4


def rwkv7_mixer(y, mu, w0, w_up, a0, a_up, k_k, k_a, r_k, ln_g, ln_b, *, bt):
  t = y.shape[0]
  width = RW_PAIRS_PER_STEP * LANES
  groups = W_MIX // width
  col = lambda name: pl.BlockSpec((bt, width), lambda g, i, c=_COL[name] // width: (i, c + g))
  vec = lambda: pl.BlockSpec((1, width), lambda g, i: (0, g))
  zeros = jnp.zeros((RW_LORA, W_MIX), F32)
  wup_p = jnp.concatenate([w_up, zeros], axis=0)
  aup_p = jnp.concatenate([zeros, a_up], axis=0)
  row = lambda p: p.reshape(1, -1)
  return pl.pallas_call(
      _rwkv_kernel,
      grid=(groups, t // bt),
      in_specs=[
          col("A_r"), col("A_k"), col("A_v"),
          pl.BlockSpec((bt, LANES), lambda g, i, c=_COL["A_lo"] // LANES: (i, c)),
          col("A_z"),
          vec(), vec(), vec(), pl.BlockSpec((1, LANES), lambda g, i: (0, 0)),
          vec(), vec(), vec(), vec(), vec(), vec(), vec(),
          pl.BlockSpec((LANES, width), lambda g, i: (0, g)),
          pl.BlockSpec((LANES, width), lambda g, i: (0, g)),
      ],
      out_specs=pl.BlockSpec((bt, width), lambda g, i: (i, g)),
      out_shape=jax.ShapeDtypeStruct((t, W_MIX), BF16),
      scratch_shapes=[pltpu.VMEM((RW_PAIRS_PER_STEP, LANES, LANES), F32),
                      pltpu.VMEM((3, SUBLANES, width), F32),
                      pltpu.VMEM((1, SUBLANES, LANES), F32)],
      compiler_params=pltpu.CompilerParams(
          dimension_semantics=("parallel", "arbitrary"), vmem_limit_bytes=VMEM_LIMIT),
      name="rwkv7_mixer",
  )(y, y, y, y, y, row(mu[:W_MIX]), row(mu[W_MIX:2 * W_MIX]), row(mu[2 * W_MIX:3 * W_MIX]),
    row(mu[3 * W_MIX:]), row(w0), row(a0), row(k_k), row(k_a), row(r_k), row(ln_g), row(ln_b),
    wup_p, aup_p)


def _gdn_kernel(q_ref, k_ref, v_ref, g_ref, z_ref, cq_ref, ck_ref, cv_ref, alog_ref, dtb_ref,
                ng_ref, o_ref, s_ref, prev_ref):
  bt, width = q_ref.shape
  n = CHUNK
  n_heads = width // LANES
  n_chunks = bt // n
  first_head = pl.program_id(0) * n_heads

  @pl.when(pl.program_id(1) == 0)
  def _():
    s_ref[...] = jnp.zeros_like(s_ref)
    prev_ref[...] = jnp.zeros_like(prev_ref)

  def l2n(x):
    return x * lax.rsqrt(jnp.sum(x * x, axis=-1, keepdims=True) + L2_EPS)

  def head(x, j):
    return x[:, j * LANES:(j + 1) * LANES]

  q_all = _causal_conv_silu(q_ref, cq_ref, prev_ref, 0)
  k_all = _causal_conv_silu(k_ref, ck_ref, prev_ref, 1)
  v_all = _causal_conv_silu(v_ref, cv_ref, prev_ref, 2)
  gates = g_ref[...]
  beta_all = _sigmoid(gates)
  log_a_all = -jnp.exp(alog_ref[...]) * _softplus(gates + dtb_ref[...])
  strict, incl = _chunk_masks(bt)
  g_all = _mask_dot(jnp.where(incl, 1.0, 0.0), log_a_all, 2)
  g_all_t = g_all.T

  qs, ks, vs, kbs, g_cols, decays = [], [], [], [], [], []
  for j in range(n_heads):
    beta = _select_lane(beta_all, first_head + j)
    g_col = _select_lane(g_all, GD_H + first_head + j)
    g_row = _select_row(g_all_t, GD_H + first_head + j)
    decays.append(jnp.exp(jnp.minimum(g_col - g_row, 0.0)))
    k = l2n(head(k_all, j))
    qs.append(l2n(head(q_all, j)) * (GD_HD ** -0.5))
    ks.append(k)
    kbs.append(k * beta)
    vs.append(head(v_all, j) * beta)
    g_cols.append(g_col)
  heads = range(n_heads)
  big = [_dot_nt(jnp.concatenate([kbs[j], qs[j]], axis=0), ks[j]) for j in heads]
  m = [jnp.where(strict, big[j][:bt] * decays[j], 0.0) for j in heads]
  attn = [jnp.where(incl, big[j][bt:] * decays[j], 0.0) for j in heads]
  t_inv = _inv_unit_lower(m)
  e_g = [jnp.exp(g) for g in g_cols]
  yx = [_dot(t_inv[j], jnp.concatenate([vs[j], kbs[j] * e_g[j]], axis=1)) for j in heads]
  qe = [qs[j] * e_g[j] for j in heads]

  s = [s_ref[j] for j in heads]
  outs = [[] for _ in heads]
  for ci in range(n_chunks):
    sl = slice(ci * n, (ci + 1) * n)
    reads = [_dot(jnp.concatenate([yx[j][sl, LANES:], qe[j][sl]], axis=0), s[j]) for j in heads]
    for j in heads:
      v_new = yx[j][sl, :LANES] - reads[j][:n]
      outs[j].append(reads[j][n:] + _dot(attn[j][sl], _rows_of_chunk(v_new, ci, bt)))
      g_end = g_cols[j][(ci + 1) * n - 1:(ci + 1) * n]
      s[j] = jnp.exp(g_end) * s[j] + _dot_tn(ks[j][sl] * jnp.exp(g_end - g_cols[j][sl]), v_new)
  for j in heads:
    s_ref[j] = s[j]
  o = jnp.concatenate(
      [_head_rms(jnp.concatenate(outs[j], axis=0), head(ng_ref[...], j)) for j in heads], axis=1)
  o_ref[...] = (o * _silu(z_ref[...])).astype(o_ref.dtype)


GD_HEADS_PER_STEP = 4


def gdn_mixer(y, conv_w, a_log, dt_bias, norm_g, *, bt):
  t = y.shape[0]
  width = GD_HEADS_PER_STEP * LANES
  groups = W_MIX // width
  col = lambda name: pl.BlockSpec((bt, width), lambda g, i, c=_COL[name] // width: (i, c + g))
  cw = lambda part: pl.BlockSpec((CONV_K, width), lambda g, i, p=part: (0, p * groups + g))
  fix = pl.BlockSpec((1, LANES), lambda g, i: (0, 0))
  on_gate_lanes = lambda p: jnp.zeros((1, LANES), F32).at[0, GD_H:2 * GD_H].set(p)
  return pl.pallas_call(
      _gdn_kernel,
      grid=(groups, t // bt),
      in_specs=[
          col("D_q"), col("D_k"), col("D_v"),
          pl.BlockSpec((bt, LANES), lambda g, i, c=_COL["D_g"] // LANES: (i, c)),
          col("D_z"), cw(0), cw(1), cw(2), fix, fix,
          pl.BlockSpec((1, width), lambda g, i: (0, g)),
      ],
      out_specs=pl.BlockSpec((bt, width), lambda g, i: (i, g)),
      out_shape=jax.ShapeDtypeStruct((t, W_MIX), BF16),
      scratch_shapes=[pltpu.VMEM((GD_HEADS_PER_STEP, GD_HD, GD_HD), F32),
                      pltpu.VMEM((3, SUBLANES, width), F32)],
      compiler_params=pltpu.CompilerParams(
          dimension_semantics=("parallel", "arbitrary"), vmem_limit_bytes=VMEM_LIMIT),
      name="gdn_mixer",
  )(y, y, y, y, y, conv_w, conv_w, conv_w, on_gate_lanes(a_log), on_gate_lanes(dt_bias),
    norm_g.reshape(1, -1))


def _mlstm_kernel(q_ref, k_ref, v_ref, og_ref, z_ref, g_ref, cq_ref, ck_ref, ib_ref, fb_ref,
                  ng_ref, o_ref, c_ref, n_ref, m_ref, prev_ref):
  bt = q_ref.shape[0]
  n = CHUNK
  h = pl.program_id(0)

  @pl.when(pl.program_id(1) == 0)
  def _():
    c_ref[...] = jnp.zeros_like(c_ref)
    n_ref[...] = jnp.zeros_like(n_ref)
    m_ref[...] = jnp.zeros_like(m_ref)
    prev_ref[...] = jnp.zeros_like(prev_ref)

  q = _causal_conv_silu(q_ref, cq_ref, prev_ref, 0)
  k = _causal_conv_silu(k_ref, ck_ref, prev_ref, 1) * (ML_HD ** -0.5)
  v = v_ref[...]
  gates = g_ref[...]
  log_i = _select_lane(gates, h) + ib_ref[...]
  log_f = -_softplus(-(_select_lane(gates, ML_H + h) + fb_ref[...]))

  tr, tc = _tri(n)
  causal = tc <= tr
  lower_incl = jnp.where(causal, 1.0, 0.0)
  upper_incl = jnp.where(tr <= tc, 1.0, 0.0)
  eye = jnp.where(tr == tc, 1.0, 0.0)
  ones = jnp.ones((n, n), F32)

  c_st, n_st, m_st = c_ref[...], n_ref[...], m_ref[...]
  outs = []
  for ci in range(bt // n):
    sl = slice(ci * n, (ci + 1) * n)
    fc, ic = log_f[sl], log_i[sl]
    g_col = _mask_dot(lower_incl, fc)
    g_row = _mask_dot(ones, upper_incl * fc[:, :n])
    i_row = _mask_dot(ones, eye * ic[:, :n])
    dmat = jnp.where(causal, g_col[:, :n] - g_row + i_row, -jnp.inf)
    inter = g_col + m_st
    m_row = jnp.maximum(jnp.max(dmat, axis=-1, keepdims=True), inter)
    qc, kc, vc = q[sl], k[sl], v[sl]
    sc = _dot_nt(qc, kc) * jnp.exp(dmat - m_row[:, :n])
    w_inter = jnp.exp(inter - m_row)[:, :1]
    num = _dot(sc, vc) + w_inter * _dot(qc, c_st)
    den = (jnp.sum(sc, axis=-1, keepdims=True)
           + w_inter * jnp.sum(qc * n_st, axis=-1, keepdims=True))
    outs.append(num / jnp.maximum(jnp.abs(den), jnp.exp(-m_row[:, :1])))
    g_end = g_col[n - 1:n]
    log_w = g_end - g_col + ic
    m_new = jnp.maximum(g_end + m_st, jnp.max(log_w, axis=0, keepdims=True))
    carry = jnp.exp(g_end + m_st - m_new)[:, :1]
    wk = jnp.exp(log_w - m_new)[:, :1] * kc
    c_st = carry * c_st + _dot_tn(wk, vc)
    n_st = carry * n_st + jnp.sum(wk, axis=0, keepdims=True)
    m_st = m_new
  c_ref[...] = c_st
  n_ref[...] = n_st
  m_ref[...] = m_st
  hh = _head_rms(jnp.concatenate(outs, axis=0), ng_ref[...]) * _sigmoid(og_ref[...])
  o_ref[...] = (hh * _silu(z_ref[...])).astype(o_ref.dtype)


def mlstm_mixer(y, conv_w, i_bias, f_bias, norm_g, *, bt):
  t = y.shape[0]
  cb = lambda name: _COL[name] // ML_HD
  col = lambda name: pl.BlockSpec((bt, ML_HD), lambda h, i, c=cb(name): (i, c + h))
  cw = lambda part: pl.BlockSpec((CONV_K, ML_HD), lambda h, i, p=part: (0, p * ML_H + h))
  scal = lambda: pl.BlockSpec((None, 1, LANES), lambda h, i: (h, 0, 0))
  bcast = lambda p: jnp.broadcast_to(p[:, None, None], (ML_H, 1, LANES))
  return pl.pallas_call(
      _mlstm_kernel,
      grid=(ML_H, t // bt),
      in_specs=[
          col("B_q"), col("B_k"), col("B_v"), col("B_o"), col("B_z"),
          pl.BlockSpec((bt, LANES), lambda h, i, c=_COL["B_g"] // LANES: (i, c)),
          cw(0), cw(1), scal(), scal(),
          pl.BlockSpec((1, ML_HD), lambda h, i: (0, h)),
      ],
      out_specs=pl.BlockSpec((bt, ML_HD), lambda h, i: (i, h)),
      out_shape=jax.ShapeDtypeStruct((t, W_MIX), BF16),
      scratch_shapes=[pltpu.VMEM((ML_HD, ML_HD), F32), pltpu.VMEM((1, ML_HD), F32),
                      pltpu.VMEM((1, LANES), F32), pltpu.VMEM((2, SUBLANES, ML_HD), F32)],
      compiler_params=pltpu.CompilerParams(
          dimension_semantics=("parallel", "arbitrary"), vmem_limit_bytes=VMEM_LIMIT),
      name="mlstm_mixer",
  )(y, y, y, y, y, y, conv_w, conv_w, bcast(i_bias), bcast(f_bias), norm_g.reshape(1, -1))


HG_SUB = 16


def _hgrn2_kernel(q_ref, f_ref, i_ref, z_ref, lb_ref, ng_ref, o_ref, st_ref):
  bt = q_ref.shape[0]
  n = CHUNK
  n_chunks = bt // n

  @pl.when(pl.program_id(1) == 0)
  def _():
    st_ref[...] = jnp.zeros_like(st_ref)

  lb = lb_ref[...]
  f = lb + (1.0 - lb) * _sigmoid(f_ref[...])
  q = _silu(q_ref[...])
  k = 1.0 - f
  log_f = jnp.log(f)
  v = i_ref[...]

  def group_row(x, size, r):
    x3 = x.reshape(bt // size, size, HG_HD)
    return jnp.broadcast_to(x3[:, r:r + 1, :], x3.shape).reshape(bt, HG_HD)

  strict, incl = _chunk_masks(bt)
  b = _mask_dot(jnp.where(incl, 1.0, 0.0), log_f, 2)
  b_ex = b - log_f
  row = _iota((bt, HG_HD), 0)
  row_in_block = row & (HG_SUB - 1)
  row_in_chunk = row & (n - 1)

  o = jnp.zeros((bt, HG_HD), F32)
  for l in range(HG_SUB):
    e = jnp.exp(jnp.where(row_in_block >= l, b - group_row(b, HG_SUB, l), -jnp.inf))
    o = o + (jnp.sum(q * group_row(k, HG_SUB, l) * e, axis=-1, keepdims=True)
             * group_row(v, HG_SUB, l))

  qs, ks = [], []
  for bi in range(1, n // HG_SUB):
    start = bi * HG_SUB
    ref = group_row(b_ex, n, start)
    in_block = (row_in_chunk >= start) & (row_in_chunk < start + HG_SUB)
    qs.append(q * jnp.exp(jnp.where(in_block, b - ref, -jnp.inf)))
    ks.append(k * jnp.exp(jnp.where(row_in_chunk < start, ref - b, -jnp.inf)))
  a_off = _dot_nt(jnp.concatenate(qs, axis=1), jnp.concatenate(ks, axis=1))
  o = o + _dot(jnp.where(strict, a_off, 0.0), v)

  qe = q * jnp.exp(b)
  kd = k * jnp.exp(group_row(b, n, n - 1) - b)
  st = st_ref[...]
  reads = []
  for ci in range(n_chunks):
    sl = slice(ci * n, (ci + 1) * n)
    reads.append(_dot_nt(qe[sl], st))
    st = st * jnp.exp(b[(ci + 1) * n - 1:(ci + 1) * n]) + _dot_tn(v[sl], kd[sl])
  st_ref[...] = st
  o = o + jnp.concatenate(reads, axis=0)
  o_ref[...] = (_head_rms(o, ng_ref[...]) * _silu(z_ref[...])).astype(o_ref.dtype)


def hgrn2_mixer(y, lower_bound, norm_g, *, bt):
  t = y.shape[0]
  cb = lambda name: _COL[name] // LANES
  col = lambda name: pl.BlockSpec((bt, LANES), lambda h, i, c=cb(name): (i, c + h))
  vec = lambda: pl.BlockSpec((1, LANES), lambda h, i: (0, h))
  return pl.pallas_call(
      _hgrn2_kernel,
      grid=(HG_H, t // bt),
      in_specs=[col("C_q"), col("C_f"), col("C_i"), col("C_z"), vec(), vec()],
      out_specs=pl.BlockSpec((bt, LANES), lambda h, i: (i, h)),
      out_shape=jax.ShapeDtypeStruct((t, W_MIX), BF16),
      scratch_shapes=[pltpu.VMEM((HG_HD, HG_HD), F32)],
      compiler_params=pltpu.CompilerParams(
          dimension_semantics=("parallel", "arbitrary"), vmem_limit_bytes=VMEM_LIMIT),
      name="hgrn2_mixer",
  )(y, y, y, y, lower_bound.reshape(1, -1), norm_g.reshape(1, -1))


def _lower_bounds_kernel(x_ref, o_ref):
  x = x_ref[...]
  e = jnp.exp(x - jnp.max(x, axis=0, keepdims=True))
  p = e / jnp.sum(e, axis=0, keepdims=True)
  acc = jnp.zeros_like(p[0:1])
  for l in range(DEPTH):
    o_ref[l:l + 1, :] = acc
    if l + 1 < DEPTH:
      acc = acc + p[l + 1:l + 2]


def hgrn2_lower_bounds(hg_lower):
  return pl.pallas_call(
      _lower_bounds_kernel,
      out_shape=jax.ShapeDtypeStruct(hg_lower.shape, F32),
      name="hgrn2_lower_bounds",
  )(hg_lower)


def _ada_kernel(c_ref, w_ref, b_ref, o_ref):
  @pl.when(pl.program_id(1) == 0)
  def _():
    o_ref[...] = b_ref[...]

  ca = _silu(c_ref[...])
  n_out = w_ref.shape[1]
  cols = [jnp.sum(w_ref[:, j * LANES:(j + 1) * LANES] * ca, axis=0, keepdims=True)
          for j in range(n_out // LANES)]
  o_ref[...] += jnp.concatenate(cols, axis=1)


def adaln_params(c, w_ada, b_ada, *, tk=128):
  depth, d, n_out = w_ada.shape
  c_cols = jnp.broadcast_to(c.reshape(d, 1), (d, LANES))
  return pl.pallas_call(
      _ada_kernel,
      grid=(depth, d // tk),
      in_specs=[
          pl.BlockSpec((tk, LANES), lambda l, k: (k, 0)),
          pl.BlockSpec((None, tk, n_out), lambda l, k: (l, k, 0)),
          pl.BlockSpec((None, 1, n_out), lambda l, k: (l, 0, 0)),
      ],
      out_specs=pl.BlockSpec((None, 1, n_out), lambda l, k: (l, 0, 0)),
      out_shape=jax.ShapeDtypeStruct((depth, 1, n_out), F32),
      compiler_params=pltpu.CompilerParams(
          dimension_semantics=("parallel", "arbitrary"), vmem_limit_bytes=VMEM_LIMIT),
      name="adaln_params",
  )(c_cols, w_ada, b_ada.reshape(depth, 1, n_out))


def _norm_mod_kernel(x_ref, g_ref, shift_ref, scale_ref, o_ref):
  x = x_ref[...]
  y = x * lax.rsqrt(jnp.mean(x * x, axis=-1, keepdims=True) + NORM_EPS) * g_ref[...]
  o_ref[...] = (y * (1.0 + scale_ref[...]) + shift_ref[...]).astype(o_ref.dtype)


def norm_modulate(x, g, shift, scale, *, tm=256):
  t, d = x.shape
  vec = pl.BlockSpec((1, d), lambda i: (0, 0))
  return pl.pallas_call(
      _norm_mod_kernel,
      grid=(t // tm,),
      in_specs=[pl.BlockSpec((tm, d), lambda i: (i, 0)), vec, vec, vec],
      out_specs=pl.BlockSpec((tm, d), lambda i: (i, 0)),
      out_shape=jax.ShapeDtypeStruct((t, d), BF16),
      compiler_params=pltpu.CompilerParams(
          dimension_semantics=("parallel",), vmem_limit_bytes=VMEM_LIMIT),
      name="norm_modulate",
  )(x, g.reshape(1, d), shift, scale)


def _final_norm_kernel(x_ref, g_ref, o_ref):
  x = x_ref[...]
  o_ref[...] = x * lax.rsqrt(jnp.mean(x * x, axis=-1, keepdims=True) + NORM_EPS) * g_ref[...]


def final_norm(x, g, *, tm=256):
  t, d = x.shape
  return pl.pallas_call(
      _final_norm_kernel,
      grid=(t // tm,),
      in_specs=[pl.BlockSpec((tm, d), lambda i: (i, 0)), pl.BlockSpec((1, d), lambda i: (0, 0))],
      out_specs=pl.BlockSpec((tm, d), lambda i: (i, 0)),
      out_shape=jax.ShapeDtypeStruct((t, d), F32),
      compiler_params=pltpu.CompilerParams(
          dimension_semantics=("parallel",), vmem_limit_bytes=VMEM_LIMIT),
      name="final_norm",
  )(x, g.reshape(1, d))


def _in_proj_kernel(h_ref, w_ref, o_ref):
  o_ref[...] = jnp.dot(h_ref[...], w_ref[...], preferred_element_type=F32)


def in_proj(h, w_all, layer, *, tm=512, tn=1280):
  t, d = h.shape
  n_out = w_all.shape[2]
  return pl.pallas_call(
      _in_proj_kernel,
      grid=(n_out // tn, t // tm),
      in_specs=[pl.BlockSpec((tm, d), lambda j, i: (i, 0)),
                pl.BlockSpec((None, d, tn), lambda j, i: (layer, 0, j))],
      out_specs=pl.BlockSpec((tm, tn), lambda j, i: (i, j)),
      out_shape=jax.ShapeDtypeStruct((t, n_out), F32),
      compiler_params=pltpu.CompilerParams(
          dimension_semantics=("parallel", "parallel"), vmem_limit_bytes=VMEM_LIMIT),
      name="in_proj",
  )(h, w_all)


def _out_proj_kernel(oa_ref, ob_ref, oc_ref, od_ref, w_ref, x_ref, gate_ref, o_ref, wb_ref):
  @pl.when(pl.program_id(1) == 0)
  def _():
    wb_ref[...] = w_ref[...].astype(BF16)

  acc = jnp.dot(oa_ref[...], wb_ref[0:W_MIX, :], preferred_element_type=F32)
  for g, ref in enumerate((ob_ref, oc_ref, od_ref), start=1):
    acc = acc + jnp.dot(ref[...], wb_ref[g * W_MIX:(g + 1) * W_MIX, :],
                        preferred_element_type=F32)
  o_ref[...] = x_ref[...] + gate_ref[...] * acc


def out_proj_residual(o_groups, w_all, layer, x, gate, *, tm=512, tn=1024):
  t, d = x.shape
  og = pl.BlockSpec((tm, W_MIX), lambda j, i: (i, 0))
  return pl.pallas_call(
      _out_proj_kernel,
      grid=(d // tn, t // tm),
      in_specs=[og, og, og, og,
                pl.BlockSpec((None, 4 * W_MIX, tn), lambda j, i: (layer, 0, j),
                             pipeline_mode=pl.Buffered(1)),
                pl.BlockSpec((tm, tn), lambda j, i: (i, j)),
                pl.BlockSpec((1, tn), lambda j, i: (0, j))],
      out_specs=pl.BlockSpec((tm, tn), lambda j, i: (i, j)),
      out_shape=jax.ShapeDtypeStruct((t, d), F32),
      scratch_shapes=[pltpu.VMEM((4 * W_MIX, tn), BF16)],
      compiler_params=pltpu.CompilerParams(
          dimension_semantics=("parallel", "arbitrary"), vmem_limit_bytes=VMEM_LIMIT),
      name="out_proj_residual",
  )(*o_groups, w_all, x, gate)


MIXER_BT = 256


def kernel(x, c, norm_g, w_ada, b_ada, w_in, w_out, rw_mu, rw_w0, rw_w_up, rw_a0, rw_a_up, rw_k_k,
           rw_k_a, rw_r_k, rw_ln_g, rw_ln_b, ml_conv, ml_i_bias, ml_f_bias, ml_norm_g, hg_lower,
           hg_norm_g, gd_conv, gd_A_log, gd_dt_bias, gd_norm_g, final_g):
  bsz, seq, d = x.shape
  assert bsz == 1 and d == D_MODEL
  xs = x.reshape(seq, d)
  lower_bounds = hgrn2_lower_bounds(hg_lower)
  ada = adaln_params(c, w_ada, b_ada)
  w_in_p = relayout_in_proj_weights(w_in)
  bt = MIXER_BT
  for l in range(DEPTH):
    shift, scale, gate = ada[l, :, :d], ada[l, :, d:2 * d], ada[l, :, 2 * d:]
    h = norm_modulate(xs, norm_g[l], shift, scale)
    y = in_proj(h, w_in_p, l)
    o_a = rwkv7_mixer(y, rw_mu[l], rw_w0[l], rw_w_up[l], rw_a0[l], rw_a_up[l], rw_k_k[l],
                      rw_k_a[l], rw_r_k[l], rw_ln_g[l], rw_ln_b[l], bt=bt)
    o_b = mlstm_mixer(y, ml_conv[l], ml_i_bias[l], ml_f_bias[l], ml_norm_g[l], bt=2 * bt)
    o_c = hgrn2_mixer(y, lower_bounds[l], hg_norm_g[l], bt=2 * bt)
    o_d = gdn_mixer(y, gd_conv[l], gd_A_log[l], gd_dt_bias[l], gd_norm_g[l], bt=bt)
    xs = out_proj_residual((o_a, o_b, o_c, o_d), w_out, l, xs, gate)
  return final_norm(xs, final_g).reshape(bsz, seq, d)
```

```python
import functools

import jax
import jax.numpy as jnp
from jax import lax
from jax.experimental import pallas as pl
from jax.experimental.pallas import tpu as pltpu

F32 = jnp.float32
BF16 = jnp.bfloat16

D_MODEL = 4096
DEPTH = 4
W_MIX = 1024
RW_HEAD = 64
RW_LORA = 64
RW_DECAY_SCALE = 0.606531
RW_GN_EPS = 64e-5
ML_H = 4
ML_HD = 256
HG_H = 8
HG_HD = 128
GD_H = 8
GD_HD = 128
CONV_K = 4
CHUNK = 64
_LOG2_CHUNK = 6
_LOG2_RW_HEAD = 6
assert 1 << _LOG2_CHUNK == CHUNK and 1 << _LOG2_RW_HEAD == RW_HEAD
NORM_EPS = 1e-6
L2_EPS = 1e-6

LANES = 128
SUBLANES = 8
VMEM_LIMIT = 48 * 1024 * 1024

_COL = dict(
    B_q=0, B_k=1024, B_v=2048, B_o=3072, B_z=4096,
    C_q=5120, C_f=6144, C_i=7168, C_z=8192,
    A_r=9216, A_k=10240, A_v=11264, A_z=12288,
    D_q=13312, D_k=14336, D_v=15360, D_z=16384,
    A_lo=17408, B_g=17536, D_g=17664,
)
P_PAD = 17920
_A0, _B0, _C0, _D0 = 0, 4224, 9352, 13448
_SRC = (
    ("B_q", _B0, 1024), ("B_k", _B0 + 1024, 1024), ("B_v", _B0 + 2048, 1024),
    ("B_o", _B0 + 3072, 1024), ("B_z", _B0 + 4104, 1024),
    ("C_q", _C0, 1024), ("C_f", _C0 + 1024, 1024), ("C_i", _C0 + 2048, 1024),
    ("C_z", _C0 + 3072, 1024),
    ("A_r", _A0, 1024), ("A_k", _A0 + 1024, 1024), ("A_v", _A0 + 2048, 1024),
    ("A_z", _A0 + 3200, 1024),
    ("D_q", _D0, 1024), ("D_k", _D0 + 1024, 1024), ("D_v", _D0 + 2048, 1024),
    ("D_z", _D0 + 3088, 1024),
    ("A_lo", _A0 + 3072, 128), ("B_g", _B0 + 4096, 8), ("D_g", _D0 + 3072, 16),
)


def relayout_columns(w):
  parts, pos = [], 0
  for name, src, width in _SRC:
    assert _COL[name] == pos, (name, pos)
    parts.append(w[..., src:src + width])
    pad = (-width) % LANES
    if pad:
      parts.append(jnp.zeros(w.shape[:-1] + (pad,), w.dtype))
    pos += width + pad
  parts.append(jnp.zeros(w.shape[:-1] + (P_PAD - pos,), w.dtype))
  return jnp.concatenate(parts, axis=-1)


RELAY_TILE = 512
RELAY_K = 4096
_N_BIG = 17
_N_BIG_TILES = _N_BIG * (1024 // RELAY_TILE)
assert _COL["A_lo"] == _N_BIG_TILES * RELAY_TILE and P_PAD == (_N_BIG_TILES + 1) * RELAY_TILE


def _relayout_kernel(src_ref, w_ref, tail_ref, o_ref):
  t = pl.program_id(1)

  @pl.when(t < _N_BIG_TILES)
  def _():
    o_ref[...] = w_ref[...].T.astype(o_ref.dtype)

  @pl.when(t >= _N_BIG_TILES)
  def _():
    o_ref[...] = tail_ref[...].T.astype(o_ref.dtype)


def relayout_in_proj_weights(w_in):
  depth, d, _ = w_in.shape
  wt = jnp.swapaxes(w_in, 1, 2)
  starts = [src + half * RELAY_TILE for _, src, _ in _SRC[:_N_BIG]
            for half in range(1024 // RELAY_TILE)]
  assert all(s % SUBLANES == 0 for s in starts)
  src_rows8 = jnp.asarray([s // SUBLANES for s in starts], jnp.int32)
  parts, pos = [], 0
  for _, src, width in _SRC[_N_BIG:]:
    parts.append(wt[:, src:src + width])
    pad = (-width) % LANES
    if pad:
      parts.append(jnp.zeros((depth, pad, d), wt.dtype))
    pos += width + pad
  parts.append(jnp.zeros((depth, RELAY_TILE - pos, d), wt.dtype))
  tail = jnp.concatenate(parts, axis=1)
  return pl.pallas_call(
      _relayout_kernel,
      grid_spec=pltpu.PrefetchScalarGridSpec(
          num_scalar_prefetch=1,
          grid=(depth, _N_BIG_TILES + 1, d // RELAY_K),
          in_specs=[
              pl.BlockSpec((None, pl.Element(RELAY_TILE), pl.Element(RELAY_K)),
                           lambda l, t, k, src: (
                               l, src[jnp.minimum(t, _N_BIG_TILES - 1)] * SUBLANES, k * RELAY_K)),
              pl.BlockSpec((None, RELAY_TILE, RELAY_K), lambda l, t, k, src: (l, 0, k)),
          ],
          out_specs=pl.BlockSpec((None, RELAY_K, RELAY_TILE), lambda l, t, k, src: (l, k, t))),
      out_shape=jax.ShapeDtypeStruct((depth, d, P_PAD), BF16),
      compiler_params=pltpu.CompilerParams(
          dimension_semantics=("parallel", "parallel", "parallel"),
          vmem_limit_bytes=VMEM_LIMIT),
      name="relayout_in_proj_weights",
  )(src_rows8, wt, tail)


def _iota(shape, dim):
  return lax.broadcasted_iota(jnp.int32, shape, dim)


def _dot(a, b):
  return jnp.dot(a.astype(BF16), b.astype(BF16), preferred_element_type=F32)


def _dot_nt(a, b):
  return lax.dot_general(a.astype(BF16), b.astype(BF16), (((1,), (1,)), ((), ())),
                         preferred_element_type=F32)


def _dot_tn(a, b):
  return lax.dot_general(a.astype(BF16), b.astype(BF16), (((0,), (0,)), ((), ())),
                         preferred_element_type=F32)


def _bf16_terms(x, n_terms=3):
  terms, rest = [], x
  for _ in range(n_terms):
    t = rest.astype(BF16)
    terms.append(t)
    rest = rest - t.astype(F32)
  return terms


def _mask_dot(mask, x, n_terms=3):
  m = mask.astype(BF16)
  return functools.reduce(
      jnp.add, [jnp.dot(m, t, preferred_element_type=F32) for t in _bf16_terms(x, n_terms)])


def _sigmoid(x):
  return 1.0 / (1.0 + jnp.exp(-x))


def _silu(x):
  return x * _sigmoid(x)


def _softplus(x):
  return jnp.maximum(x, 0.0) + jnp.log1p(jnp.exp(-jnp.abs(x)))


def _shift_rows(x, prev8, j):
  rolled = pltpu.roll(x, j, 0)
  top = jnp.where(_iota((SUBLANES, x.shape[1]), 0) < j, pltpu.roll(prev8, j, 0),
                  rolled[:SUBLANES])
  return jnp.concatenate([top, rolled[SUBLANES:]], axis=0)


def _causal_conv_silu(x_ref, w_ref, prev_ref, slot):
  x = x_ref[...]
  prev8 = prev_ref[slot]
  acc = x * w_ref[CONV_K - 1:CONV_K, :]
  for j in range(1, CONV_K):
    acc = acc + _shift_rows(x, prev8, j) * w_ref[CONV_K - 1 - j:CONV_K - j, :]
  prev_ref[slot] = x[x.shape[0] - SUBLANES:]
  return _silu(acc)


def _tri(n):
  r, c = _iota((n, n), 0), _iota((n, n), 1)
  return r, c


def _chunk_masks(rows):
  r, c = _tri(rows)
  same = (r >> _LOG2_CHUNK) == (c >> _LOG2_CHUNK)
  return same & (c < r), same & (c <= r)


def _inv_unit_lower(mats):
  n = mats[0].shape[0]
  r, c = _tri(n)
  same = lambda log2_size: (r >> log2_size) == (c >> log2_size)
  base = 3
  mm = lambda x, y: jnp.dot(x.astype(BF16), y.astype(BF16), preferred_element_type=F32)
  diag = [jnp.where(same(base), a, 0.0).astype(BF16) for a in mats]
  xs = [jnp.where(r == c, 1.0, 0.0) - d for d in diag]
  ps = [mm(d, d) for d in diag]
  xs = [x + mm(x, p) for x, p in zip(xs, ps)]
  ps = [mm(p, p) for p in ps]
  xs = [x + mm(x, p) for x, p in zip(xs, ps)]
  for level in range(base, _LOG2_CHUNK):
    sibling = same(level + 1) & jnp.logical_not(same(level))
    offs = [jnp.where(sibling, a, 0.0) for a in mats]
    half = [mm(x, o) for x, o in zip(xs, offs)]
    xs = [x - mm(h, x) for x, h in zip(xs, half)]
  return xs


def _select_lane(x, lane_idx):
  return jnp.sum(jnp.where(_iota(x.shape, 1) == lane_idx, x, 0.0), axis=-1, keepdims=True)


def _select_row(x, row_idx):
  return jnp.sum(jnp.where(_iota(x.shape, 0) == row_idx, x, 0.0), axis=0, keepdims=True)


def _rows_of_chunk(x, ci, total_rows):
  parts = []
  if ci:
    parts.append(jnp.zeros((ci * CHUNK, x.shape[1]), x.dtype))
  parts.append(x)
  rest = total_rows - (ci + 1) * CHUNK
  if rest:
    parts.append(jnp.zeros((rest, x.shape[1]), x.dtype))
  return jnp.concatenate(parts, axis=0)


def _head_rms(o, g):
  return o * lax.rsqrt(jnp.mean(o * o, axis=-1, keepdims=True) + NORM_EPS) * g


def _rwkv_kernel(r_ref, k_ref, v_ref, lo_ref, z_ref, mur_ref, muk_ref, muv_ref, mulo_ref,
                 w0_ref, a0_ref, kkw_ref, kaw_ref, rkw_ref, lng_ref, lnb_ref, wup_ref, aup_ref,
                 o_ref, s_ref, prev_ref, prevlo_ref):
  bt, width = r_ref.shape
  n = CHUNK
  n_pairs = width // LANES
  n_chunks = bt // n

  @pl.when(pl.program_id(1) == 0)
  def _():
    s_ref[...] = jnp.zeros_like(s_ref)
    prev_ref[...] = jnp.zeros_like(prev_ref)
    prevlo_ref[...] = jnp.zeros_like(prevlo_ref)

  def token_shift(x_ref, mu_ref, carry_ref, slot):
    x = x_ref[...]
    prev = _shift_rows(x, carry_ref[slot], 1)
    carry_ref[slot] = x[bt - SUBLANES:]
    return x + (prev - x) * mu_ref[...]

  r = token_shift(r_ref, mur_ref, prev_ref, 0)
  k = token_shift(k_ref, muk_ref, prev_ref, 1)
  v = token_shift(v_ref, muv_ref, prev_ref, 2)
  lo = token_shift(lo_ref, mulo_ref, prevlo_ref, 0)

  lo_act = jnp.where(_iota((bt, LANES), 1) < RW_LORA, jnp.tanh(lo), lo)
  log_w = -RW_DECAY_SCALE * _sigmoid(w0_ref[...] + _dot(lo_act, wup_ref[...]))
  a = _sigmoid(a0_ref[...] + _dot(lo_act, aup_ref[...]))

  hr, hc = _tri(width)
  head_ones = jnp.where(hr >> _LOG2_RW_HEAD == hc >> _LOG2_RW_HEAD, 1.0, 0.0)
  kkp = k * kkw_ref[...]
  k2 = k * (1.0 + (a - 1.0) * kaw_ref[...])
  sums = _dot(jnp.concatenate([kkp * kkp, r * k2 * rkw_ref[...]], axis=0), head_ones)
  kk = kkp * lax.rsqrt(sums[:bt] + L2_EPS)
  b = kk * a
  bonus = sums[bt:] * v

  strict, incl = _chunk_masks(bt)
  lc = _mask_dot(jnp.where(incl, 1.0, 0.0), log_w, 2)
  g_in = jnp.exp(lc)
  g_inv = jnp.exp(-lc)
  kap_all = kk * jnp.exp(lc - log_w)
  rt_all = r * g_in
  kh_all = k2 * g_inv
  bh_all = b * g_inv

  lane = _iota((bt, LANES), 1)
  lane_n = _iota((n, LANES), 1)
  head_masks = [jnp.where(lane < RW_HEAD, 1.0, 0.0), jnp.where(lane >= RW_HEAD, 1.0, 0.0)]
  chunk_head_masks = [jnp.where(lane_n < RW_HEAD, 1.0, 0.0), jnp.where(lane_n >= RW_HEAD, 1.0, 0.0)]
  pr, pc = _tri(LANES)
  pair_diag = (pr >> _LOG2_RW_HEAD) == (pc >> _LOG2_RW_HEAD)

  def pair(x, p):
    return x[:, p * LANES:(p + 1) * LANES]

  vheads = [(p, h) for p in range(n_pairs) for h in range(2)]
  big = [_dot_nt(jnp.concatenate([pair(kap_all, p), pair(rt_all, p)], axis=0),
                 jnp.concatenate([pair(kh_all, p) * head_masks[h],
                                  pair(bh_all, p) * head_masks[h]], axis=0))
         for p, h in vheads]
  a_kk = [jnp.where(strict, m[:bt, :bt], 0.0) for m in big]
  a_bk = [jnp.where(strict, m[:bt, bt:], 0.0) for m in big]
  a_kr = [jnp.where(incl, m[bt:, :bt], 0.0) for m in big]
  a_br = [jnp.where(incl, m[bt:, bt:], 0.0) for m in big]
  t_inv = _inv_unit_lower(a_bk)
  av = [_dot(jnp.concatenate([a_kk[i], a_kr[i]], axis=0), pair(v, p) * head_masks[h])
        for i, (p, h) in enumerate(vheads)]
  xy = [_dot(t_inv[i], jnp.concatenate([pair(kap_all, p) * head_masks[h], av[i][:bt]], axis=1))
        for i, (p, h) in enumerate(vheads)]
  x_p = [xy[2 * p][:, :LANES] + xy[2 * p + 1][:, :LANES] for p in range(n_pairs)]
  y_p = [xy[2 * p][:, LANES:] + xy[2 * p + 1][:, LANES:] for p in range(n_pairs)]
  akrv_p = [av[2 * p][bt:] + av[2 * p + 1][bt:] for p in range(n_pairs)]

  s = [s_ref[p] for p in range(n_pairs)]
  outs = [[] for _ in range(n_pairs)]
  for ci in range(n_chunks):
    sl = slice(ci * n, (ci + 1) * n)
    g_last = [pair(g_in, p)[(ci + 1) * n - 1:(ci + 1) * n] for p in range(n_pairs)]
    reads = [_dot_nt(jnp.concatenate([x_p[p][sl], pair(rt_all, p)[sl]], axis=0), s[p])
             for p in range(n_pairs)]
    u = [reads[p][:n] + y_p[p][sl] for p in range(n_pairs)]
    for p in range(n_pairs):
      u_rows = jnp.concatenate(
          [_rows_of_chunk(u[p] * chunk_head_masks[h], ci, bt) for h in range(2)], axis=0)
      a_br_rows = jnp.concatenate([a_br[2 * p][sl], a_br[2 * p + 1][sl]], axis=1)
      outs[p].append(reads[p][n:] + akrv_p[p][sl] - _dot(a_br_rows, u_rows))
      delta = _dot_tn(jnp.concatenate([pair(v, p)[sl], u[p]], axis=0),
                      jnp.concatenate([pair(kh_all, p)[sl] * g_last[p],
                                       -pair(bh_all, p)[sl] * g_last[p]], axis=0))
      s[p] = s[p] * g_last[p] + jnp.where(pair_diag, delta, 0.0)
  for p in range(n_pairs):
    s_ref[p] = s[p]
  o = jnp.concatenate([jnp.concatenate(outs[p], axis=0) for p in range(n_pairs)], axis=1)

  inv_n = 1.0 / RW_HEAD
  mean = _dot(o, head_ones) * inv_n
  d = o - mean
  var = _dot(d * d, head_ones) * inv_n
  o = d * lax.rsqrt(var + RW_GN_EPS) * lng_ref[...] + lnb_ref[...] + bonus
  o_ref[...] = (o * _silu(z_ref[...])).astype(o_ref.dtype)


RW_PAIRS_PER_STEP = 8


def rwkv7_mixer(y, mu, w0, w_up, a0, a_up, k_k, k_a, r_k, ln_g, ln_b, *, bt):
  t = y.shape[0]
  width = RW_PAIRS_PER_STEP * LANES
  groups = W_MIX // width
  col = lambda name: pl.BlockSpec((bt, width), lambda g, i, c=_COL[name] // width: (i, c + g))
  vec = lambda: pl.BlockSpec((1, width), lambda g, i: (0, g))
  zeros = jnp.zeros((RW_LORA, W_MIX), F32)
  wup_p = jnp.concatenate([w_up, zeros], axis=0)
  aup_p = jnp.concatenate([zeros, a_up], axis=0)
  row = lambda p: p.reshape(1, -1)
  return pl.pallas_call(
      _rwkv_kernel,
      grid=(groups, t // bt),
      in_specs=[
          col("A_r"), col("A_k"), col("A_v"),
          pl.BlockSpec((bt, LANES), lambda g, i, c=_COL["A_lo"] // LANES: (i, c)),
          col("A_z"),
          vec(), vec(), vec(), pl.BlockSpec((1, LANES), lambda g, i: (0, 0)),
          vec(), vec(), vec(), vec(), vec(), vec(), vec(),
          pl.BlockSpec((LANES, width), lambda g, i: (0, g)),
          pl.BlockSpec((LANES, width), lambda g, i: (0, g)),
      ],
      out_specs=pl.BlockSpec((bt, width), lambda g, i: (i, g)),
      out_shape=jax.ShapeDtypeStruct((t, W_MIX), BF16),
      scratch_shapes=[pltpu.VMEM((RW_PAIRS_PER_STEP, LANES, LANES), F32),
                      pltpu.VMEM((3, SUBLANES, width), F32),
                      pltpu.VMEM((1, SUBLANES, LANES), F32)],
      compiler_params=pltpu.CompilerParams(
          dimension_semantics=("parallel", "arbitrary"), vmem_limit_bytes=VMEM_LIMIT),
      name="rwkv7_mixer",
  )(y, y, y, y, y, row(mu[:W_MIX]), row(mu[W_MIX:2 * W_MIX]), row(mu[2 * W_MIX:3 * W_MIX]),
    row(mu[3 * W_MIX:]), row(w0), row(a0), row(k_k), row(k_a), row(r_k), row(ln_g), row(ln_b),
    wup_p, aup_p)


def _gdn_kernel(q_ref, k_ref, v_ref, g_ref, z_ref, cq_ref, ck_ref, cv_ref, alog_ref, dtb_ref,
                ng_ref, o_ref, s_ref, prev_ref):
  bt, width = q_ref.shape
  n = CHUNK
  n_heads = width // LANES
  n_chunks = bt // n
  first_head = pl.program_id(0) * n_heads

  @pl.when(pl.program_id(1) == 0)
  def _():
    s_ref[...] = jnp.zeros_like(s_ref)
    prev_ref[...] = jnp.zeros_like(prev_ref)

  def l2n(x):
    return x * lax.rsqrt(jnp.sum(x * x, axis=-1, keepdims=True) + L2_EPS)

  def head(x, j):
    return x[:, j * LANES:(j + 1) * LANES]

  q_all = _causal_conv_silu(q_ref, cq_ref, prev_ref, 0)
  k_all = _causal_conv_silu(k_ref, ck_ref, prev_ref, 1)
  v_all = _causal_conv_silu(v_ref, cv_ref, prev_ref, 2)
  gates = g_ref[...]
  beta_all = _sigmoid(gates)
  log_a_all = -jnp.exp(alog_ref[...]) * _softplus(gates + dtb_ref[...])
  strict, incl = _chunk_masks(bt)
  g_all = _mask_dot(jnp.where(incl, 1.0, 0.0), log_a_all, 2)
  g_all_t = g_all.T

  qs, ks, vs, kbs, g_cols, decays = [], [], [], [], [], []
  for j in range(n_heads):
    beta = _select_lane(beta_all, first_head + j)
    g_col = _select_lane(g_all, GD_H + first_head + j)
    g_row = _select_row(g_all_t, GD_H + first_head + j)
    decays.append(jnp.exp(jnp.minimum(g_col - g_row, 0.0)))
    k = l2n(head(k_all, j))
    qs.append(l2n(head(q_all, j)) * (GD_HD ** -0.5))
    ks.append(k)
    kbs.append(k * beta)
    vs.append(head(v_all, j) * beta)
    g_cols.append(g_col)
  heads = range(n_heads)
  big = [_dot_nt(jnp.concatenate([kbs[j], qs[j]], axis=0), ks[j]) for j in heads]
  m = [jnp.where(strict, big[j][:bt] * decays[j], 0.0) for j in heads]
  attn = [jnp.where(incl, big[j][bt:] * decays[j], 0.0) for j in heads]
  t_inv = _inv_unit_lower(m)
  e_g = [jnp.exp(g) for g in g_cols]
  yx = [_dot(t_inv[j], jnp.concatenate([vs[j], kbs[j] * e_g[j]], axis=1)) for j in heads]
  qe = [qs[j] * e_g[j] for j in heads]

  s = [s_ref[j] for j in heads]
  outs = [[] for _ in heads]
  for ci in range(n_chunks):
    sl = slice(ci * n, (ci + 1) * n)
    reads = [_dot(jnp.concatenate([yx[j][sl, LANES:], qe[j][sl]], axis=0), s[j]) for j in heads]
    for j in heads:
      v_new = yx[j][sl, :LANES] - reads[j][:n]
      outs[j].append(reads[j][n:] + _dot(attn[j][sl], _rows_of_chunk(v_new, ci, bt)))
      g_end = g_cols[j][(ci + 1) * n - 1:(ci + 1) * n]
      s[j] = jnp.exp(g_end) * s[j] + _dot_tn(ks[j][sl] * jnp.exp(g_end - g_cols[j][sl]), v_new)
  for j in heads:
    s_ref[j] = s[j]
  o = jnp.concatenate(
      [_head_rms(jnp.concatenate(outs[j], axis=0), head(ng_ref[...], j)) for j in heads], axis=1)
  o_ref[...] = (o * _silu(z_ref[...])).astype(o_ref.dtype)


GD_HEADS_PER_STEP = 8


def gdn_mixer(y, conv_w, a_log, dt_bias, norm_g, *, bt):
  t = y.shape[0]
  width = GD_HEADS_PER_STEP * LANES
  groups = W_MIX // width
  col = lambda name: pl.BlockSpec((bt, width), lambda g, i, c=_COL[name] // width: (i, c + g))
  cw = lambda part: pl.BlockSpec((CONV_K, width), lambda g, i, p=part: (0, p * groups + g))
  fix = pl.BlockSpec((1, LANES), lambda g, i: (0, 0))
  on_gate_lanes = lambda p: jnp.zeros((1, LANES), F32).at[0, GD_H:2 * GD_H].set(p)
  return pl.pallas_call(
      _gdn_kernel,
      grid=(groups, t // bt),
      in_specs=[
          col("D_q"), col("D_k"), col("D_v"),
          pl.BlockSpec((bt, LANES), lambda g, i, c=_COL["D_g"] // LANES: (i, c)),
          col("D_z"), cw(0), cw(1), cw(2), fix, fix,
          pl.BlockSpec((1, width), lambda g, i: (0, g)),
      ],
      out_specs=pl.BlockSpec((bt, width), lambda g, i: (i, g)),
      out_shape=jax.ShapeDtypeStruct((t, W_MIX), BF16),
      scratch_shapes=[pltpu.VMEM((GD_HEADS_PER_STEP, GD_HD, GD_HD), F32),
                      pltpu.VMEM((3, SUBLANES, width), F32)],
      compiler_params=pltpu.CompilerParams(
          dimension_semantics=("parallel", "arbitrary"), vmem_limit_bytes=VMEM_LIMIT),
      name="gdn_mixer",
  )(y, y, y, y, y, conv_w, conv_w, conv_w, on_gate_lanes(a_log), on_gate_lanes(dt_bias),
    norm_g.reshape(1, -1))


def _mlstm_kernel(q_ref, k_ref, v_ref, og_ref, z_ref, g_ref, cq_ref, ck_ref, ib_ref, fb_ref,
                  ng_ref, o_ref, c_ref, n_ref, m_ref, prev_ref):
  bt = q_ref.shape[0]
  n = CHUNK
  h = pl.program_id(0)

  @pl.when(pl.program_id(1) == 0)
  def _():
    c_ref[...] = jnp.zeros_like(c_ref)
    n_ref[...] = jnp.zeros_like(n_ref)
    m_ref[...] = jnp.zeros_like(m_ref)
    prev_ref[...] = jnp.zeros_like(prev_ref)

  q = _causal_conv_silu(q_ref, cq_ref, prev_ref, 0)
  k = _causal_conv_silu(k_ref, ck_ref, prev_ref, 1) * (ML_HD ** -0.5)
  v = v_ref[...]
  gates = g_ref[...]
  log_i = _select_lane(gates, h) + ib_ref[...]
  log_f = -_softplus(-(_select_lane(gates, ML_H + h) + fb_ref[...]))

  tr, tc = _tri(n)
  causal = tc <= tr
  lower_incl = jnp.where(causal, 1.0, 0.0)
  upper_incl = jnp.where(tr <= tc, 1.0, 0.0)
  eye = jnp.where(tr == tc, 1.0, 0.0)
  ones = jnp.ones((n, n), F32)

  c_st, n_st, m_st = c_ref[...], n_ref[...], m_ref[...]
  outs = []
  for ci in range(bt // n):
    sl = slice(ci * n, (ci + 1) * n)
    fc, ic = log_f[sl], log_i[sl]
    g_col = _mask_dot(lower_incl, fc)
    g_row = _mask_dot(ones, upper_incl * fc[:, :n])
    i_row = _mask_dot(ones, eye * ic[:, :n])
    dmat = jnp.where(causal, g_col[:, :n] - g_row + i_row, -jnp.inf)
    inter = g_col + m_st
    m_row = jnp.maximum(jnp.max(dmat, axis=-1, keepdims=True), inter)
    qc, kc, vc = q[sl], k[sl], v[sl]
    sc = _dot_nt(qc, kc) * jnp.exp(dmat - m_row[:, :n])
    w_inter = jnp.exp(inter - m_row)[:, :1]
    num = _dot(sc, vc) + w_inter * _dot(qc, c_st)
    den = (jnp.sum(sc, axis=-1, keepdims=True)
           + w_inter * jnp.sum(qc * n_st, axis=-1, keepdims=True))
    outs.append(num / jnp.maximum(jnp.abs(den), jnp.exp(-m_row[:, :1])))
    g_end = g_col[n - 1:n]
    log_w = g_end - g_col + ic
    m_new = jnp.maximum(g_end + m_st, jnp.max(log_w, axis=0, keepdims=True))
    carry = jnp.exp(g_end + m_st - m_new)[:, :1]
    wk = jnp.exp(log_w - m_new)[:, :1] * kc
    c_st = carry * c_st + _dot_tn(wk, vc)
    n_st = carry * n_st + jnp.sum(wk, axis=0, keepdims=True)
    m_st = m_new
  c_ref[...] = c_st
  n_ref[...] = n_st
  m_ref[...] = m_st
  hh = _head_rms(jnp.concatenate(outs, axis=0), ng_ref[...]) * _sigmoid(og_ref[...])
  o_ref[...] = (hh * _silu(z_ref[...])).astype(o_ref.dtype)


def mlstm_mixer(y, conv_w, i_bias, f_bias, norm_g, *, bt):
  t = y.shape[0]
  cb = lambda name: _COL[name] // ML_HD
  col = lambda name: pl.BlockSpec((bt, ML_HD), lambda h, i, c=cb(name): (i, c + h))
  cw = lambda part: pl.BlockSpec((CONV_K, ML_HD), lambda h, i, p=part: (0, p * ML_H + h))
  scal = lambda: pl.BlockSpec((None, 1, LANES), lambda h, i: (h, 0, 0))
  bcast = lambda p: jnp.broadcast_to(p[:, None, None], (ML_H, 1, LANES))
  return pl.pallas_call(
      _mlstm_kernel,
      grid=(ML_H, t // bt),
      in_specs=[
          col("B_q"), col("B_k"), col("B_v"), col("B_o"), col("B_z"),
          pl.BlockSpec((bt, LANES), lambda h, i, c=_COL["B_g"] // LANES: (i, c)),
          cw(0), cw(1), scal(), scal(),
          pl.BlockSpec((1, ML_HD), lambda h, i: (0, h)),
      ],
      out_specs=pl.BlockSpec((bt, ML_HD), lambda h, i: (i, h)),
      out_shape=jax.ShapeDtypeStruct((t, W_MIX), BF16),
      scratch_shapes=[pltpu.VMEM((ML_HD, ML_HD), F32), pltpu.VMEM((1, ML_HD), F32),
                      pltpu.VMEM((1, LANES), F32), pltpu.VMEM((2, SUBLANES, ML_HD), F32)],
      compiler_params=pltpu.CompilerParams(
          dimension_semantics=("parallel", "arbitrary"), vmem_limit_bytes=VMEM_LIMIT),
      name="mlstm_mixer",
  )(y, y, y, y, y, y, conv_w, conv_w, bcast(i_bias), bcast(f_bias), norm_g.reshape(1, -1))


HG_SUB = 16


def _hgrn2_kernel(q_ref, f_ref, i_ref, z_ref, lb_ref, ng_ref, o_ref, st_ref):
  bt = q_ref.shape[0]
  n = CHUNK
  n_chunks = bt // n

  @pl.when(pl.program_id(1) == 0)
  def _():
    st_ref[...] = jnp.zeros_like(st_ref)

  lb = lb_ref[...]
  f = lb + (1.0 - lb) * _sigmoid(f_ref[...])
  q = _silu(q_ref[...])
  k = 1.0 - f
  log_f = jnp.log(f)
  v = i_ref[...]

  def group_row(x, size, r):
    x3 = x.reshape(bt // size, size, HG_HD)
    return jnp.broadcast_to(x3[:, r:r + 1, :], x3.shape).reshape(bt, HG_HD)

  strict, incl = _chunk_masks(bt)
  b = _mask_dot(jnp.where(incl, 1.0, 0.0), log_f, 2)
  b_ex = b - log_f
  row = _iota((bt, HG_HD), 0)
  row_in_block = row & (HG_SUB - 1)
  row_in_chunk = row & (n - 1)

  o = jnp.zeros((bt, HG_HD), F32)
  for l in range(HG_SUB):
    e = jnp.exp(jnp.where(row_in_block >= l, b - group_row(b, HG_SUB, l), -jnp.inf))
    o = o + (jnp.sum(q * group_row(k, HG_SUB, l) * e, axis=-1, keepdims=True)
             * group_row(v, HG_SUB, l))

  qs, ks = [], []
  for bi in range(1, n // HG_SUB):
    start = bi * HG_SUB
    ref = group_row(b_ex, n, start)
    in_block = (row_in_chunk >= start) & (row_in_chunk < start + HG_SUB)
    qs.append(q * jnp.exp(jnp.where(in_block, b - ref, -jnp.inf)))
    ks.append(k * jnp.exp(jnp.where(row_in_chunk < start, ref - b, -jnp.inf)))
  a_off = _dot_nt(jnp.concatenate(qs, axis=1), jnp.concatenate(ks, axis=1))
  o = o + _dot(jnp.where(strict, a_off, 0.0), v)

  qe = q * jnp.exp(b)
  kd = k * jnp.exp(group_row(b, n, n - 1) - b)
  st = st_ref[...]
  reads = []
  for ci in range(n_chunks):
    sl = slice(ci * n, (ci + 1) * n)
    reads.append(_dot_nt(qe[sl], st))
    st = st * jnp.exp(b[(ci + 1) * n - 1:(ci + 1) * n]) + _dot_tn(v[sl], kd[sl])
  st_ref[...] = st
  o = o + jnp.concatenate(reads, axis=0)
  o_ref[...] = (_head_rms(o, ng_ref[...]) * _silu(z_ref[...])).astype(o_ref.dtype)


def hgrn2_mixer(y, lower_bound, norm_g, *, bt):
  t = y.shape[0]
  cb = lambda name: _COL[name] // LANES
  col = lambda name: pl.BlockSpec((bt, LANES), lambda h, i, c=cb(name): (i, c + h))
  vec = lambda: pl.BlockSpec((1, LANES), lambda h, i: (0, h))
  return pl.pallas_call(
      _hgrn2_kernel,
      grid=(HG_H, t // bt),
      in_specs=[col("C_q"), col("C_f"), col("C_i"), col("C_z"), vec(), vec()],
      out_specs=pl.BlockSpec((bt, LANES), lambda h, i: (i, h)),
      out_shape=jax.ShapeDtypeStruct((t, W_MIX), BF16),
      scratch_shapes=[pltpu.VMEM((HG_HD, HG_HD), F32)],
      compiler_params=pltpu.CompilerParams(
          dimension_semantics=("parallel", "arbitrary"), vmem_limit_bytes=VMEM_LIMIT),
      name="hgrn2_mixer",
  )(y, y, y, y, lower_bound.reshape(1, -1), norm_g.reshape(1, -1))


def _lower_bounds_kernel(x_ref, o_ref):
  x = x_ref[...]
  e = jnp.exp(x - jnp.max(x, axis=0, keepdims=True))
  p = e / jnp.sum(e, axis=0, keepdims=True)
  acc = jnp.zeros_like(p[0:1])
  for l in range(DEPTH):
    o_ref[l:l + 1, :] = acc
    if l + 1 < DEPTH:
      acc = acc + p[l + 1:l + 2]


def hgrn2_lower_bounds(hg_lower):
  return pl.pallas_call(
      _lower_bounds_kernel,
      out_shape=jax.ShapeDtypeStruct(hg_lower.shape, F32),
      name="hgrn2_lower_bounds",
  )(hg_lower)


def _ada_kernel(c_ref, w_ref, b_ref, o_ref):
  @pl.when(pl.program_id(1) == 0)
  def _():
    o_ref[...] = b_ref[...]

  ca = _silu(c_ref[...])
  n_out = w_ref.shape[1]
  cols = [jnp.sum(w_ref[:, j * LANES:(j + 1) * LANES] * ca, axis=0, keepdims=True)
          for j in range(n_out // LANES)]
  o_ref[...] += jnp.concatenate(cols, axis=1)


def adaln_params(c, w_ada, b_ada, *, tk=128):
  depth, d, n_out = w_ada.shape
  c_cols = jnp.broadcast_to(c.reshape(d, 1), (d, LANES))
  return pl.pallas_call(
      _ada_kernel,
      grid=(depth, d // tk),
      in_specs=[
          pl.BlockSpec((tk, LANES), lambda l, k: (k, 0)),
          pl.BlockSpec((None, tk, n_out), lambda l, k: (l, k, 0)),
          pl.BlockSpec((None, 1, n_out), lambda l, k: (l, 0, 0)),
      ],
      out_specs=pl.BlockSpec((None, 1, n_out), lambda l, k: (l, 0, 0)),
      out_shape=jax.ShapeDtypeStruct((depth, 1, n_out), F32),
      compiler_params=pltpu.CompilerParams(
          dimension_semantics=("parallel", "arbitrary"), vmem_limit_bytes=VMEM_LIMIT),
      name="adaln_params",
  )(c_cols, w_ada, b_ada.reshape(depth, 1, n_out))


def _norm_mod_kernel(x_ref, g_ref, shift_ref, scale_ref, o_ref):
  x = x_ref[...]
  y = x * lax.rsqrt(jnp.mean(x * x, axis=-1, keepdims=True) + NORM_EPS) * g_ref[...]
  o_ref[...] = (y * (1.0 + scale_ref[...]) + shift_ref[...]).astype(o_ref.dtype)


def norm_modulate(x, g, shift, scale, *, tm=256):
  t, d = x.shape
  vec = pl.BlockSpec((1, d), lambda i: (0, 0))
  return pl.pallas_call(
      _norm_mod_kernel,
      grid=(t // tm,),
      in_specs=[pl.BlockSpec((tm, d), lambda i: (i, 0)), vec, vec, vec],
      out_specs=pl.BlockSpec((tm, d), lambda i: (i, 0)),
      out_shape=jax.ShapeDtypeStruct((t, d), BF16),
      compiler_params=pltpu.CompilerParams(
          dimension_semantics=("parallel",), vmem_limit_bytes=VMEM_LIMIT),
      name="norm_modulate",
  )(x, g.reshape(1, d), shift, scale)


def _final_norm_kernel(x_ref, g_ref, o_ref):
  x = x_ref[...]
  o_ref[...] = x * lax.rsqrt(jnp.mean(x * x, axis=-1, keepdims=True) + NORM_EPS) * g_ref[...]


def final_norm(x, g, *, tm=256):
  t, d = x.shape
  return pl.pallas_call(
      _final_norm_kernel,
      grid=(t // tm,),
      in_specs=[pl.BlockSpec((tm, d), lambda i: (i, 0)), pl.BlockSpec((1, d), lambda i: (0, 0))],
      out_specs=pl.BlockSpec((tm, d), lambda i: (i, 0)),
      out_shape=jax.ShapeDtypeStruct((t, d), F32),
      compiler_params=pltpu.CompilerParams(
          dimension_semantics=("parallel",), vmem_limit_bytes=VMEM_LIMIT),
      name="final_norm",
  )(x, g.reshape(1, d))


def _in_proj_kernel(h_ref, w_ref, o_ref):
  o_ref[...] = jnp.dot(h_ref[...], w_ref[...], preferred_element_type=F32)


def in_proj(h, w_all, layer, *, tm=512, tn=1280):
  t, d = h.shape
  n_out = w_all.shape[2]
  return pl.pallas_call(
      _in_proj_kernel,
      grid=(n_out // tn, t // tm),
      in_specs=[pl.BlockSpec((tm, d), lambda j, i: (i, 0)),
                pl.BlockSpec((None, d, tn), lambda j, i: (layer, 0, j))],
      out_specs=pl.BlockSpec((tm, tn), lambda j, i: (i, j)),
      out_shape=jax.ShapeDtypeStruct((t, n_out), F32),
      compiler_params=pltpu.CompilerParams(
          dimension_semantics=("parallel", "parallel"), vmem_limit_bytes=VMEM_LIMIT),
      name="in_proj",
  )(h, w_all)


def _out_proj_kernel(oa_ref, ob_ref, oc_ref, od_ref, w_ref, x_ref, gate_ref, o_ref, wb_ref):
  @pl.when(pl.program_id(1) == 0)
  def _():
    wb_ref[...] = w_ref[...].astype(BF16)

  acc = jnp.dot(oa_ref[...], wb_ref[0:W_MIX, :], preferred_element_type=F32)
  for g, ref in enumerate((ob_ref, oc_ref, od_ref), start=1):
    acc = acc + jnp.dot(ref[...], wb_ref[g * W_MIX:(g + 1) * W_MIX, :],
                        preferred_element_type=F32)
  o_ref[...] = x_ref[...] + gate_ref[...] * acc


def out_proj_residual(o_groups, w_all, layer, x, gate, *, tm=512, tn=1024):
  t, d = x.shape
  og = pl.BlockSpec((tm, W_MIX), lambda j, i: (i, 0))
  return pl.pallas_call(
      _out_proj_kernel,
      grid=(d // tn, t // tm),
      in_specs=[og, og, og, og,
                pl.BlockSpec((None, 4 * W_MIX, tn), lambda j, i: (layer, 0, j),
                             pipeline_mode=pl.Buffered(1)),
                pl.BlockSpec((tm, tn), lambda j, i: (i, j)),
                pl.BlockSpec((1, tn), lambda j, i: (0, j))],
      out_specs=pl.BlockSpec((tm, tn), lambda j, i: (i, j)),
      out_shape=jax.ShapeDtypeStruct((t, d), F32),
      scratch_shapes=[pltpu.VMEM((4 * W_MIX, tn), BF16)],
      compiler_params=pltpu.CompilerParams(
          dimension_semantics=("parallel", "arbitrary"), vmem_limit_bytes=VMEM_LIMIT),
      name="out_proj_residual",
  )(*o_groups, w_all, x, gate)


MIXER_BT = 256


def kernel(x, c, norm_g, w_ada, b_ada, w_in, w_out, rw_mu, rw_w0, rw_w_up, rw_a0, rw_a_up, rw_k_k,
           rw_k_a, rw_r_k, rw_ln_g, rw_ln_b, ml_conv, ml_i_bias, ml_f_bias, ml_norm_g, hg_lower,
           hg_norm_g, gd_conv, gd_A_log, gd_dt_bias, gd_norm_g, final_g):
  bsz, seq, d = x.shape
  assert bsz == 1 and d == D_MODEL
  xs = x.reshape(seq, d)
  lower_bounds = hgrn2_lower_bounds(hg_lower)
  ada = adaln_params(c, w_ada, b_ada)
  w_in_p = relayout_in_proj_weights(w_in)
  bt = MIXER_BT
  for l in range(DEPTH):
    shift, scale, gate = ada[l, :, :d], ada[l, :, d:2 * d], ada[l, :, 2 * d:]
    h = norm_modulate(xs, norm_g[l], shift, scale)
    y = in_proj(h, w_in_p, l)
    o_a = rwkv7_mixer(y, rw_mu[l], rw_w0[l], rw_w_up[l], rw_a0[l], rw_a_up[l], rw_k_k[l],
                      rw_k_a[l], rw_r_k[l], rw_ln_g[l], rw_ln_b[l], bt=bt)
    o_b = mlstm_mixer(y, ml_conv[l], ml_i_bias[l], ml_f_bias[l], ml_norm_g[l], bt=2 * bt)
    o_c = hgrn2_mixer(y, lower_bounds[l], hg_norm_g[l], bt=2 * bt)
    o_d = gdn_mixer(y, gd_conv[l], gd_A_log[l], gd_dt_bias[l], gd_norm_g[l], bt=bt)
    xs = out_proj_residual((o_a, o_b, o_c, o_d), w_out, l, xs, gate)
  return final_norm(xs, final_g).reshape(bsz, seq, d)
```
